```python
import jax, jax.numpy as jnp
from jax import lax
import numpy as np

D_MODEL = 1024
BATCH = 8
SEQ = 4096
DEPTH = 2
DEC_BATCH = 32
DEC_SEQ = 16
PAST_LEN = 2048

CHUNK = 64
N_EVEN = (DEPTH + 1) // 2
N_ODD = DEPTH // 2
GMLP_CHUNK = 128
A_GROUP_DIM = 128
D_A = D_MODEL // 2
A_GROUPS = D_A // A_GROUP_DIM
D_B = D_MODEL // 2
CONV_W = 3
C_HEAD_DIM = 64
C_HEADS = D_MODEL // C_HEAD_DIM
D_C = C_HEADS * C_HEAD_DIM
Q_BLOCK = 128
D_FF = 11 * D_MODEL // 4
EPS = 1e-6
NEG_INF = -1e30

kernel_name = "chunk_streaming_gmlp_conv_fox_trunk"


def _rmsnorm(x, g):
    xf = x.astype(jnp.float32)
    y = xf * lax.rsqrt(jnp.mean(xf * xf, axis=-1, keepdims=True) + EPS)
    return (y * g.astype(jnp.float32)).astype(x.dtype)


def _causal_dwconv(x, w, prev):
    t = x.shape[1]
    xp = jnp.concatenate([prev.astype(x.dtype), x], axis=1)
    y = xp[:, 0:t] * w[0]
    for i in range(1, CONV_W):
        y = y + xp[:, i:i + t] * w[i]
    return y, xp[:, t:]


def _chunk_mask(n):
    pos = jnp.arange(n)
    return (pos[None, :] // CHUNK) <= (pos[:, None] // CHUNK)


def _mixer_ab(x, conv_prev, w_in, v_norm, w_s, b_s, conv_w, w_out):
    b, t, _ = x.shape
    z = x @ w_in
    u, v, g_b, g_c, x_b = jnp.split(z, [D_A, 2 * D_A, 2 * D_A + D_B, 2 * D_A + 2 * D_B], axis=-1)
    u = jax.nn.gelu(u)
    v = _rmsnorm(jax.nn.gelu(v).reshape(b, t, A_GROUPS, A_GROUP_DIM), v_norm)
    n = -(-t // GMLP_CHUNK)
    vp = jnp.pad(v, ((0, 0), (0, n * GMLP_CHUNK - t), (0, 0), (0, 0)))
    vp = vp.reshape(b, n, GMLP_CHUNK, A_GROUPS, A_GROUP_DIM)
    w_m = jnp.where(_chunk_mask(GMLP_CHUNK)[None], w_s, 0.0)
    mixed = jnp.einsum('gts,bnsgc->bntgc', w_m, vp) + b_s.T[:, :, None]
    mixed = mixed.reshape(b, n * GMLP_CHUNK, D_A)[:, :t]
    y_a = u * mixed
    conv_out, conv_state = _causal_dwconv(g_c * x_b, conv_w, conv_prev)
    y_b = g_b * conv_out
    y = jnp.concatenate([y_a, y_b], axis=-1) @ w_out
    return y, conv_state, v.reshape(b, t, D_A)


def _fox_project(x, w_in, b_f, q_norm, k_norm):
    b, t, _ = x.shape
    z = x @ w_in
    q, k, v, f = jnp.split(z, [D_C, 2 * D_C, 3 * D_C], axis=-1)
    q = _rmsnorm(q.reshape(b, t, C_HEADS, C_HEAD_DIM), q_norm)
    k = _rmsnorm(k.reshape(b, t, C_HEADS, C_HEAD_DIM), k_norm)
    v = v.reshape(b, t, C_HEADS, C_HEAD_DIM)
    logf = jax.nn.log_sigmoid((f + b_f).astype(jnp.float32))
    return q, k, v, logf


def _fox_attend(q, k, v, dq, dk, q_pos, k_pos):
    s = jnp.einsum('bthd,bshd->bhts', q, k, preferred_element_type=jnp.float32) * (C_HEAD_DIM ** -0.5)
    s = s + jnp.swapaxes(dq, 1, 2)[..., :, None] - jnp.swapaxes(dk, 1, 2)[..., None, :]
    s = jnp.where(k_pos[None, :] <= q_pos[:, None], s, NEG_INF)
    p = jax.nn.softmax(s, axis=-1)
    return jnp.einsum('bhts,bshd->bthd', p.astype(v.dtype), v)


def _mixer_c_prompt(x, w_in, b_f, q_norm, k_norm, w_out):
    b, t, _ = x.shape
    q, k, v, logf = _fox_project(x, w_in, b_f, q_norm, k_norm)
    d = jnp.cumsum(logf, axis=1)
    pos = jnp.arange(t)
    nb = t // Q_BLOCK
    qb = jnp.swapaxes(q.reshape(b, nb, Q_BLOCK, C_HEADS, C_HEAD_DIM), 0, 1)
    db = jnp.swapaxes(d.reshape(b, nb, Q_BLOCK, C_HEADS), 0, 1)
    pb = pos.reshape(nb, Q_BLOCK)
    o = lax.map(lambda blk: _fox_attend(blk[0], k, v, blk[1], d, blk[2], pos), (qb, db, pb))
    o = jnp.swapaxes(o, 0, 1).reshape(b, t, D_C)
    return o @ w_out, k, v, logf


def _mixer_c_sample(x, cache_k, cache_v, cache_logf, w_in, b_f, q_norm, k_norm, w_out):
    b, t, _ = x.shape
    p_len = cache_k.shape[1]
    q, k, v, logf = _fox_project(x, w_in, b_f, q_norm, k_norm)
    c_cum = jnp.cumsum(cache_logf.astype(jnp.float32), axis=1)
    d_new = jnp.cumsum(logf, axis=1)
    dk = jnp.concatenate([c_cum - c_cum[:, -1:], d_new], axis=1)
    keys = jnp.concatenate([cache_k.astype(k.dtype), k], axis=1)
    vals = jnp.concatenate([cache_v.astype(v.dtype), v], axis=1)
    o = _fox_attend(q, keys, vals, d_new, dk, p_len + jnp.arange(t), jnp.arange(p_len + t))
    return o.reshape(b, t, D_C) @ w_out, k, v, logf


def _conv_ffn(x, prev, w_up, conv_w, w_down):
    z, st = _causal_dwconv(x @ w_up, conv_w, prev)
    gate, up = jnp.split(z, 2, axis=-1)
    return (jax.nn.silu(gate) * up) @ w_down, st


def setup_inputs(seed: int = 0) -> dict:
    key = jax.random.key(seed)
    ks = jax.random.split(key, 24)
    f32 = jnp.float32

    def nrm(k, shape, scale):
        return jax.random.normal(k, shape, f32) * scale

    return {
        "x_prompt": nrm(ks[0], (BATCH, SEQ, D_MODEL), 1.0),
        "x_sample": nrm(ks[1], (DEC_BATCH, DEC_SEQ, D_MODEL), 1.0),
        "state_conv_b": nrm(ks[2], (N_EVEN, DEC_BATCH, CONV_W - 1, D_B), 1.0),
        "state_ffn": nrm(ks[3], (DEPTH, DEC_BATCH, CONV_W - 1, 2 * D_FF), 1.0),
        "cache_k": nrm(ks[4], (N_ODD, DEC_BATCH, PAST_LEN, C_HEADS, C_HEAD_DIM), 1.0),
        "cache_v": nrm(ks[5], (N_ODD, DEC_BATCH, PAST_LEN, C_HEADS, C_HEAD_DIM), 1.0),
        "cache_logf": jax.nn.log_sigmoid(jax.random.uniform(ks[6], (N_ODD, DEC_BATCH, PAST_LEN, C_HEADS), f32, 1.0, 6.0)),
        "norm_mix": 1.0 + nrm(ks[7], (DEPTH, D_MODEL), 0.05),
        "norm_ffn": 1.0 + nrm(ks[8], (DEPTH, D_MODEL), 0.05),
        "w_in_ab": nrm(ks[9], (N_EVEN, D_MODEL, 2 * D_A + 3 * D_B), D_MODEL ** -0.5),
        "sgu_norm": 1.0 + nrm(ks[10], (N_EVEN, A_GROUPS, A_GROUP_DIM), 0.05),
        "w_spatial": nrm(ks[11], (N_EVEN, A_GROUPS, GMLP_CHUNK, GMLP_CHUNK), GMLP_CHUNK ** -0.5),
        "b_spatial": 1.0 + nrm(ks[12], (N_EVEN, A_GROUPS, GMLP_CHUNK), 0.05),
        "conv_b": nrm(ks[13], (N_EVEN, CONV_W, D_B), CONV_W ** -0.5),
        "w_out_ab": nrm(ks[14], (N_EVEN, D_A + D_B, D_MODEL), (D_A + D_B) ** -0.5),
        "w_in_c": nrm(ks[15], (N_ODD, D_MODEL, 3 * D_C + C_HEADS), D_MODEL ** -0.5),
        "b_forget": jax.random.uniform(ks[16], (N_ODD, C_HEADS), f32, 1.0, 6.0),
        "q_norm": 1.0 + nrm(ks[17], (N_ODD, C_HEAD_DIM), 0.05),
        "k_norm": 1.0 + nrm(ks[18], (N_ODD, C_HEAD_DIM), 0.05),
        "w_out_c": nrm(ks[19], (N_ODD, D_C, D_MODEL), D_C ** -0.5),
        "w_up": nrm(ks[20], (DEPTH, D_MODEL, 2 * D_FF), D_MODEL ** -0.5),
        "conv_ffn": nrm(ks[21], (DEPTH, CONV_W, 2 * D_FF), CONV_W ** -0.5),
        "w_down": nrm(ks[22], (DEPTH, D_FF, D_MODEL), D_FF ** -0.5),
    }


def reference(x_prompt, x_sample, state_conv_b, state_ffn, cache_k, cache_v, cache_logf,
              norm_mix, norm_ffn, w_in_ab, sgu_norm, w_spatial, b_spatial, conv_b, w_out_ab,
              w_in_c, b_forget, q_norm, k_norm, w_out_c, w_up, conv_ffn, w_down):
    xp, xs = x_prompt, x_sample
    conv_b_p, conv_b_s, gmlp_v_s = [], [], []
    k_p, v_p, lf_p, k_s, v_s, lf_s = [], [], [], [], [], []
    ffn_p, ffn_s = [], []
    for layer in range(DEPTH):
        i = layer // 2
        hp = _rmsnorm(xp, norm_mix[layer])
        hs = _rmsnorm(xs, norm_mix[layer])
        if layer % 2 == 0:
            zero_b = jnp.zeros((xp.shape[0], CONV_W - 1, D_B), xp.dtype)
            yp, cst_p, _ = _mixer_ab(hp, zero_b, w_in_ab[i], sgu_norm[i], w_spatial[i], b_spatial[i], conv_b[i], w_out_ab[i])
            ys, cst_s, v_new = _mixer_ab(hs, state_conv_b[i], w_in_ab[i], sgu_norm[i], w_spatial[i], b_spatial[i], conv_b[i], w_out_ab[i])
            conv_b_p.append(cst_p)
            conv_b_s.append(cst_s)
            gmlp_v_s.append(v_new)
        else:
            yp, kp_, vp_, lfp = _mixer_c_prompt(hp, w_in_c[i], b_forget[i], q_norm[i], k_norm[i], w_out_c[i])
            ys, ks_, vs_, lfs = _mixer_c_sample(hs, cache_k[i], cache_v[i], cache_logf[i], w_in_c[i], b_forget[i], q_norm[i], k_norm[i], w_out_c[i])
            k_p.append(kp_)
            v_p.append(vp_)
            lf_p.append(lfp.astype(x_prompt.dtype))
            k_s.append(ks_)
            v_s.append(vs_)
            lf_s.append(lfs.astype(x_sample.dtype))
        xp = xp + yp
        xs = xs + ys
        hp = _rmsnorm(xp, norm_ffn[layer])
        hs = _rmsnorm(xs, norm_ffn[layer])
        zero_f = jnp.zeros((xp.shape[0], CONV_W - 1, 2 * D_FF), xp.dtype)
        yp, st_p = _conv_ffn(hp, zero_f, w_up[layer], conv_ffn[layer], w_down[layer])
        ys, st_s = _conv_ffn(hs, state_ffn[layer], w_up[layer], conv_ffn[layer], w_down[layer])
        ffn_p.append(st_p)
        ffn_s.append(st_s)
        xp = xp + yp
        xs = xs + ys
    return (xp, xs,
            jnp.stack(conv_b_p), jnp.stack(conv_b_s), jnp.stack(gmlp_v_s),
            jnp.stack(k_p), jnp.stack(v_p), jnp.stack(lf_p),
            jnp.stack(k_s), jnp.stack(v_s), jnp.stack(lf_s),
            jnp.stack(ffn_p), jnp.stack(ffn_s))
```

```python
import functools

import jax
import jax.numpy as jnp
from jax import lax
from jax.experimental import pallas as pl
from jax.experimental.pallas import tpu as pltpu

F32 = jnp.float32
BF16 = jnp.bfloat16

D_MODEL = 1024
CHUNK = 64
GMLP_CHUNK = 128
A_GROUP_DIM = 128
D_A = D_MODEL // 2
A_GROUPS = D_A // A_GROUP_DIM
D_B = D_MODEL // 2
CONV_W = 3
C_HEAD_DIM = 64
C_HEADS = D_MODEL // C_HEAD_DIM
D_C = C_HEADS * C_HEAD_DIM
D_FF = 11 * D_MODEL // 4
EPS = 1e-6
NEG_INF = -1e30

LANES = 128
SUBLANES = 8
MXU_COLS = 256
VMEM_LIMIT_BYTES = 56 * 1024 * 1024
ROW_TILE = 512
FF_CHUNK = MXU_COLS
ATT_TQ = 512
ATT_TK = 512


def _params(sem):
    return pltpu.CompilerParams(dimension_semantics=sem, vmem_limit_bytes=VMEM_LIMIT_BYTES)


def _resident(shape):
    nd = len(shape)
    return pl.BlockSpec(shape, lambda *_: (0,) * nd, pipeline_mode=pl.Buffered(1))


def _rms(x, g):
    ms = jnp.mean(x * x, axis=-1, keepdims=True)
    return x * lax.rsqrt(ms + EPS) * g


def _shift_rows_carry(z, carry):
    r1 = pltpu.roll(z, 1, 0)
    r2 = pltpu.roll(z, 2, 0)
    rid = lax.broadcasted_iota(jnp.int32, (SUBLANES, z.shape[1]), 0)
    h1 = jnp.where(rid < 1, pltpu.roll(carry, 1, 0), r1[0:SUBLANES])
    h2 = jnp.where(rid < 2, pltpu.roll(carry, 2, 0), r2[0:SUBLANES])
    z1 = jnp.concatenate([h1, r1[SUBLANES:]], axis=0)
    z2 = jnp.concatenate([h2, r2[SUBLANES:]], axis=0)
    return z1, z2


def _shift_rows_segments(z, p1, p2, seg):
    rid = lax.broadcasted_iota(jnp.int32, z.shape, 0) & (seg - 1)
    z1 = jnp.where(rid >= 1, pltpu.roll(z, 1, 0), p1)
    z2 = jnp.where(rid >= 2, pltpu.roll(z, 2, 0), p2)
    return z1, z2


def _conv3(z, z1, z2, w):
    return z2 * w[0:1] + z1 * w[1:2] + z * w[2:3]


def _ab_kernel(*refs, tm, seg):
    if seg is None:
        (x_ref, g_ref, win_ref, sg_ref, wm_ref, bb_ref, cw_ref,
         y_ref, st_ref, carry_ref) = refs
    else:
        (x_ref, g_ref, win_ref, sg_ref, wm_ref, bb_ref, cw_ref, p1_ref, p2_ref,
         y_ref, c_ref, v_ref) = refs

    h = _rms(x_ref[...], g_ref[...]).astype(BF16)
    z = jnp.dot(h, win_ref[...], preferred_element_type=F32)

    ya = []
    vn = []
    for g in range(A_GROUPS):
        lo = g * A_GROUP_DIM
        u = jax.nn.gelu(z[:, lo:lo + A_GROUP_DIM])
        vg = jax.nn.gelu(z[:, D_A + lo:D_A + lo + A_GROUP_DIM])
        ms = jnp.mean(vg * vg, axis=-1, keepdims=True)
        vg = vg * lax.rsqrt(ms + EPS) * sg_ref[:, lo:lo + A_GROUP_DIM]
        vn.append(vg)
        vb = vg.astype(BF16)
        wm = wm_ref[g]
        bias = bb_ref[g]
        blocks = []
        for c in range(tm // GMLP_CHUNK):
            blk = vb[c * GMLP_CHUNK:(c + 1) * GMLP_CHUNK]
            blocks.append(jnp.dot(wm, blk, preferred_element_type=F32) + bias)
        ya.append(u * jnp.concatenate(blocks, axis=0))

    off = 2 * D_A
    g_b = z[:, off:off + D_B]
    c = z[:, off + D_B:off + 2 * D_B] * z[:, off + 2 * D_B:off + 3 * D_B]
    if seg is None:
        @pl.when(pl.program_id(1) == 0)
        def _():
            carry_ref[...] = jnp.zeros_like(carry_ref)
        c1, c2 = _shift_rows_carry(c, carry_ref[...])
        carry_ref[...] = c[tm - SUBLANES:tm]
        st_ref[...] = c[tm - SUBLANES:tm]
    else:
        c1, c2 = _shift_rows_segments(c, p1_ref[...], p2_ref[...], seg)
        c_ref[...] = c
        v_ref[...] = jnp.concatenate(vn, axis=1)
    yb = g_b * _conv3(c, c1, c2, cw_ref[...])
    y_ref[...] = jnp.concatenate(ya + [yb], axis=1).astype(BF16)


def _mixer_ab(x, g, win, sg, wm, bb, cw, seg=None, p1=None, p2=None):
    b, t, d = x.shape
    tm = min(ROW_TILE, t)
    nt = t // tm
    row = lambda bi, ti: (bi, ti, 0)
    in_specs = [pl.BlockSpec((None, tm, d), row), _resident(g.shape), _resident(win.shape),
                _resident(sg.shape), _resident(wm.shape), _resident(bb.shape), _resident(cw.shape)]
    args = [x, g, win, sg, wm, bb, cw]
    y_shape = jax.ShapeDtypeStruct((b, t, D_A + D_B), BF16)
    y_spec = pl.BlockSpec((None, tm, D_A + D_B), row)
    if seg is None:
        out_shape = (y_shape, jax.ShapeDtypeStruct((b, SUBLANES, D_B), F32))
        out_specs = (y_spec, pl.BlockSpec((None, SUBLANES, D_B), lambda bi, ti: (bi, 0, 0)))
        scratch = [pltpu.VMEM((SUBLANES, D_B), F32)]
    else:
        in_specs += [pl.BlockSpec((None, tm, D_B), row), pl.BlockSpec((None, tm, D_B), row)]
        args += [p1, p2]
        out_shape = (y_shape, jax.ShapeDtypeStruct((b, t, D_B), F32),
                     jax.ShapeDtypeStruct((b, t, D_A), F32))
        out_specs = (y_spec, pl.BlockSpec((None, tm, D_B), row), pl.BlockSpec((None, tm, D_A), row))
        scratch = []
    return pl.pallas_call(
        functools.partial(_ab_kernel, tm=tm, seg=seg),
        grid=(b, nt), in_specs=in_specs, out_specs=out_specs, out_shape=out_shape,
        scratch_shapes=scratch, compiler_params=_params(("arbitrary", "arbitrary")),
        name="mixer_ab_seg" if seg else "mixer_ab",
    )(*args)


def _post_kernel(*refs, tm, seg):
    if seg is None:
        (x_ref, y_ref, wpre_ref, g_ref, wup_ref, cw_ref, wdn_ref,
         xo_ref, st_ref, a_ref, carry_ref) = refs
    else:
        (x_ref, y_ref, wpre_ref, g_ref, wup_ref, cw_ref, wdn_ref, s0_ref, s1_ref,
         xo_ref, st0_ref, st1_ref, a_ref, p1_ref, p2_ref, zs_ref) = refs
        nseg = tm // seg

    x1 = x_ref[...] + jnp.dot(y_ref[...], wpre_ref[...], preferred_element_type=F32)
    h = _rms(x1, g_ref[...]).astype(BF16)

    if seg is None:
        @pl.when(pl.program_id(1) == 0)
        def _():
            carry_ref[...] = jnp.zeros_like(carry_ref)
    else:
        p1_ref[...] = jnp.zeros_like(p1_ref)
        p2_ref[...] = jnp.zeros_like(p2_ref)

    def conv_cols(lo):
        z = jnp.dot(h, wup_ref[:, lo:lo + FF_CHUNK], preferred_element_type=F32)
        if seg is None:
            z1, z2 = _shift_rows_carry(z, carry_ref[:, lo:lo + FF_CHUNK])
            carry_ref[:, lo:lo + FF_CHUNK] = z[tm - SUBLANES:tm]
            st_ref[:, lo:lo + FF_CHUNK] = z[tm - SUBLANES:tm]
        else:
            for i in range(FF_CHUNK // LANES):
                cols = slice(lo + i * LANES, lo + (i + 1) * LANES)
                p1_ref[i, pl.ds(0, nseg, stride=seg), :] = s1_ref[:, cols]
                p2_ref[i, pl.ds(0, nseg, stride=seg), :] = s0_ref[:, cols]
                p2_ref[i, pl.ds(1, nseg, stride=seg), :] = s1_ref[:, cols]
                zs_ref[i] = z[:, i * LANES:(i + 1) * LANES]
                st0_ref[:, cols] = zs_ref[i, pl.ds(seg - 2, nseg, stride=seg), :]
                st1_ref[:, cols] = zs_ref[i, pl.ds(seg - 1, nseg, stride=seg), :]
            p1 = jnp.concatenate([p1_ref[i] for i in range(FF_CHUNK // LANES)], axis=1)
            p2 = jnp.concatenate([p2_ref[i] for i in range(FF_CHUNK // LANES)], axis=1)
            z1, z2 = _shift_rows_segments(z, p1, p2, seg)
        return _conv3(z, z1, z2, cw_ref[:, lo:lo + FF_CHUNK])

    for j in range(D_FF // FF_CHUNK):
        gate = conv_cols(j * FF_CHUNK)
        up = conv_cols(D_FF + j * FF_CHUNK)
        a_ref[:, j * FF_CHUNK:(j + 1) * FF_CHUNK] = (jax.nn.silu(gate) * up).astype(BF16)

    xo_ref[...] = x1 + jnp.dot(a_ref[...], wdn_ref[...], preferred_element_type=F32)


def _post(x, y, wpre, g, wup, cw, wdn, seg=None, s0=None, s1=None):
    b, t, d = x.shape
    tm = min(ROW_TILE, t)
    nt = t // tm
    assert seg is None or (b == 1 and nt == 1), "packed streams must fit one row tile"
    row = lambda bi, ti: (bi, ti, 0)
    in_specs = [pl.BlockSpec((None, tm, d), row), pl.BlockSpec((None, tm, y.shape[2]), row),
                _resident(wpre.shape), _resident(g.shape), _resident(wup.shape),
                _resident(cw.shape), _resident(wdn.shape)]
    args = [x, y, wpre, g, wup, cw, wdn]
    xo_shape = jax.ShapeDtypeStruct((b, t, d), F32)
    xo_spec = pl.BlockSpec((None, tm, d), row)
    scratch = [pltpu.VMEM((tm, D_FF), BF16)]
    if seg is None:
        out_shape = (xo_shape, jax.ShapeDtypeStruct((b, SUBLANES, 2 * D_FF), F32))
        out_specs = (xo_spec, pl.BlockSpec((None, SUBLANES, 2 * D_FF), lambda bi, ti: (bi, 0, 0)))
        scratch.append(pltpu.VMEM((SUBLANES, 2 * D_FF), F32))
    else:
        in_specs += [_resident(s0.shape), _resident(s1.shape)]
        args += [s0, s1]
        st_shape = jax.ShapeDtypeStruct(s0.shape, F32)
        st_spec = pl.BlockSpec(s0.shape, lambda bi, ti: (0, 0))
        out_shape = (xo_shape, st_shape, st_shape)
        out_specs = (xo_spec, st_spec, st_spec)
        scratch += [pltpu.VMEM((FF_CHUNK // LANES, tm, LANES), F32)] * 3
    return pl.pallas_call(
        functools.partial(_post_kernel, tm=tm, seg=seg),
        grid=(b, nt), in_specs=in_specs, out_specs=out_specs, out_shape=out_shape,
        scratch_shapes=scratch, compiler_params=_params(("arbitrary", "arbitrary")),
        name="post_ffn_seg" if seg else "post_ffn",
    )(*args)


def _fox_proj_kernel(x_ref, g_ref, wqkv_ref, wf_ref, bf_ref, qg_ref, kg_ref, hm_ref,
                     q_ref, kf_ref, vf_ref, kb_ref, vb_ref, lf_ref):
    h = _rms(x_ref[...], g_ref[...]).astype(BF16)
    z = jnp.dot(h, wqkv_ref[...], preferred_element_type=F32)
    q = z[:, 0:D_C]
    k = z[:, D_C:2 * D_C]
    v = z[:, 2 * D_C:3 * D_C]
    hm = hm_ref[...]
    qms = jnp.dot((q * q).astype(BF16), hm, preferred_element_type=F32)
    kms = jnp.dot((k * k).astype(BF16), hm, preferred_element_type=F32)
    qn = q * lax.rsqrt(qms + EPS) * qg_ref[...]
    kn = k * lax.rsqrt(kms + EPS) * kg_ref[...]
    q_ref[...] = qn.astype(BF16)
    kf_ref[...] = kn
    vf_ref[...] = v
    kb_ref[...] = kn.astype(BF16)
    vb_ref[...] = v.astype(BF16)
    f = lax.dot_general(wf_ref[...], h, (((1,), (1,)), ((), ())),
                        preferred_element_type=F32) + bf_ref[...]
    lf_ref[...] = jnp.minimum(f, 0.0) - jnp.log1p(jnp.exp(-jnp.abs(f)))


def _fox_proj(x2d, g, wqkv, wft, bf, qg, kg, hm):
    n, d = x2d.shape
    tm = min(ROW_TILE, n)
    row = lambda i: (i, 0)
    full = pl.BlockSpec((tm, D_C), row)
    return pl.pallas_call(
        _fox_proj_kernel,
        grid=(n // tm,),
        in_specs=[pl.BlockSpec((tm, d), row), _resident(g.shape), _resident(wqkv.shape),
                  _resident(wft.shape), _resident(bf.shape), _resident(qg.shape),
                  _resident(kg.shape), _resident(hm.shape)],
        out_specs=(full, full, full, full, full, pl.BlockSpec((C_HEADS, tm), lambda i: (0, i))),
        out_shape=(jax.ShapeDtypeStruct((n, D_C), BF16), jax.ShapeDtypeStruct((n, D_C), F32),
                   jax.ShapeDtypeStruct((n, D_C), F32), jax.ShapeDtypeStruct((n, D_C), BF16),
                   jax.ShapeDtypeStruct((n, D_C), BF16), jax.ShapeDtypeStruct((C_HEADS, n), F32)),
        compiler_params=_params(("arbitrary",)),
        name="fox_proj",
    )(x2d, g, wqkv, wft, bf, qg, kg, hm)


def _cumsum_kernel(x_ref, tri_ref, o_ref, *, sub_last):
    r, n = x_ref.shape
    tri = tri_ref[...]
    off = jnp.zeros((r, 1), F32)
    for i in range(n // LANES):
        xb = x_ref[:, i * LANES:(i + 1) * LANES]
        hi = xb.astype(BF16)
        r1 = xb - hi.astype(F32)
        mid = r1.astype(BF16)
        lo = (r1 - mid.astype(F32)).astype(BF16)
        c = (jnp.dot(hi, tri, preferred_element_type=F32)
             + jnp.dot(mid, tri, preferred_element_type=F32)
             + jnp.dot(lo, tri, preferred_element_type=F32)) + off
        o_ref[:, i * LANES:(i + 1) * LANES] = c
        off = c[:, LANES - 1:LANES]
    if sub_last:
        o_ref[...] = o_ref[...] - off


def _cumsum_lanes(x, sub_last=False):
    tri = (jnp.arange(LANES)[:, None] <= jnp.arange(LANES)[None, :]).astype(BF16)
    return pl.pallas_call(
        functools.partial(_cumsum_kernel, sub_last=sub_last),
        out_shape=jax.ShapeDtypeStruct(x.shape, F32),
        compiler_params=pltpu.CompilerParams(vmem_limit_bytes=VMEM_LIMIT_BYTES),
        name="cumsum_lanes",
    )(x, tri)


def _att_prompt_kernel(q_ref, k_ref, v_ref, dk_ref, o_ref):
    hp = pl.program_id(1)
    qi = pl.program_id(2)
    tq = q_ref.shape[0]
    q = q_ref[...]
    lane = lax.broadcasted_iota(jnp.int32, (1, LANES), 1)
    low = lane < C_HEAD_DIM
    qh = (jnp.where(low, q, jnp.zeros_like(q)), jnp.where(low, jnp.zeros_like(q), q))

    def block(j, carry, masked):
        ks = pl.multiple_of(j * ATT_TK, ATT_TK)
        kb = k_ref[pl.ds(ks, ATT_TK), :]
        vb = v_ref[pl.ds(ks, ATT_TK), :]
        out = []
        for e in range(2):
            m, l, acc = carry[3 * e:3 * e + 3]
            s = lax.dot_general(qh[e], kb, (((1,), (1,)), ((), ())), preferred_element_type=F32)
            s = s - dk_ref[pl.ds(2 * hp + e, 1), pl.ds(ks, ATT_TK)]
            if masked:
                rows = lax.broadcasted_iota(jnp.int32, s.shape, 0)
                cols = lax.broadcasted_iota(jnp.int32, s.shape, 1)
                s = jnp.where(cols <= rows, s, NEG_INF)
            m_new = jnp.maximum(m, jnp.max(s, axis=-1, keepdims=True))
            alpha = jnp.exp(m - m_new)
            p = jnp.exp(s - m_new)
            l = alpha * l + jnp.sum(p, axis=-1, keepdims=True)
            acc = alpha * acc + jnp.dot(p.astype(BF16), vb, preferred_element_type=F32)
            out += [m_new, l, acc]
        return tuple(out)

    init = []
    for _ in range(2):
        init += [jnp.full((tq, 1), NEG_INF, F32), jnp.zeros((tq, 1), F32),
                 jnp.zeros((tq, LANES), F32)]
    carry = lax.fori_loop(0, qi, lambda j, c: block(j, c, False), tuple(init))
    carry = block(qi, carry, True)
    o0 = carry[2] / carry[1]
    o1 = carry[5] / carry[4]
    o_ref[...] = jnp.where(low, o0, o1).astype(BF16)


def _att_prompt(q, k, v, dk):
    b, t, _ = q.shape
    nq = t // ATT_TQ
    hp_count = C_HEADS // 2
    return pl.pallas_call(
        _att_prompt_kernel,
        grid=(b, hp_count, nq),
        in_specs=[pl.BlockSpec((None, ATT_TQ, LANES), lambda bi, hi, qi: (bi, qi, hi)),
                  pl.BlockSpec((None, t, LANES), lambda bi, hi, qi: (bi, 0, hi)),
                  pl.BlockSpec((None, t, LANES), lambda bi, hi, qi: (bi, 0, hi)),
                  pl.BlockSpec((None, C_HEADS, t), lambda bi, hi, qi: (bi, 0, 0))],
        out_specs=pl.BlockSpec((None, ATT_TQ, LANES), lambda bi, hi, qi: (bi, qi, hi)),
        out_shape=jax.ShapeDtypeStruct((b, t, D_C), BF16),
        compiler_params=_params(("arbitrary", "arbitrary", "arbitrary")),
        name="att_prompt",
    )(q, k, v, dk)


def _att_sample_kernel(q_ref, kc_ref, vc_ref, kn_ref, vn_ref, dkc_ref, dkn_ref, o_ref, *, t_new):
    rows = C_HEADS * t_new
    q = q_ref[...]
    qt = jnp.concatenate([q] * C_HEADS, axis=0)
    rhead = lax.broadcasted_iota(jnp.int32, (rows, D_C), 0) // t_new
    lhead = lax.broadcasted_iota(jnp.int32, (rows, D_C), 1) // C_HEAD_DIM
    own = rhead == lhead
    qrows = jnp.where(own, qt, jnp.zeros_like(qt))

    def expand(d):
        return jnp.concatenate(
            [jnp.broadcast_to(d[hh:hh + 1], (t_new, d.shape[1])) for hh in range(C_HEADS)], axis=0)

    nt = (((1,), (1,)), ((), ()))
    kc = kc_ref[...].astype(BF16)
    s_c = lax.dot_general(qrows, kc, nt, preferred_element_type=F32) - expand(dkc_ref[...])
    s_n = lax.dot_general(qrows, kn_ref[...], nt, preferred_element_type=F32)
    s_n = s_n - expand(dkn_ref[...])[:, 0:t_new]
    tpos = lax.broadcasted_iota(jnp.int32, (rows, t_new), 0) & (t_new - 1)
    spos = lax.broadcasted_iota(jnp.int32, (rows, t_new), 1)
    s_n = jnp.where(spos <= tpos, s_n, NEG_INF)

    m = jnp.maximum(jnp.max(s_c, axis=-1, keepdims=True), jnp.max(s_n, axis=-1, keepdims=True))
    p_c = jnp.exp(s_c - m)
    p_n = jnp.exp(s_n - m)
    l = jnp.sum(p_c, axis=-1, keepdims=True) + jnp.sum(p_n, axis=-1, keepdims=True)
    o_full = (jnp.dot(p_c.astype(BF16), vc_ref[...].astype(BF16), preferred_element_type=F32)
              + jnp.dot(p_n.astype(BF16), vn_ref[...], preferred_element_type=F32)) / l
    o_full = jnp.where(own, o_full, 0.0)
    o = o_full[0:t_new]
    for hh in range(1, C_HEADS):
        o = o + o_full[hh * t_new:(hh + 1) * t_new]
    o_ref[...] = o.astype(BF16)


def _att_sample(q, kc, vc, kn, vn, dkc, dkn):
    b, t_new, _ = q.shape
    p_len = kc.shape[1]
    per = lambda bi: (bi, 0, 0)
    return pl.pallas_call(
        functools.partial(_att_sample_kernel, t_new=t_new),
        grid=(b,),
        in_specs=[pl.BlockSpec((None, t_new, D_C), per), pl.BlockSpec((None, p_len, D_C), per),
                  pl.BlockSpec((None, p_len, D_C), per), pl.BlockSpec((None, t_new, D_C), per),
                  pl.BlockSpec((None, t_new, D_C), per), pl.BlockSpec((None, C_HEADS, p_len), per),
                  pl.BlockSpec((None, C_HEADS, LANES), per)],
        out_specs=pl.BlockSpec((None, t_new, D_C), per),
        out_shape=jax.ShapeDtypeStruct((b, t_new, D_C), BF16),
        compiler_params=_params(("arbitrary",)),
        name="att_sample",
    )(q, kc, vc, kn, vn, dkc, dkn)


def _expand_state(state, t):
    b, _, c = state.shape
    p1 = jnp.zeros((b, t, c), F32).at[:, 0].set(state[:, 1])
    p2 = jnp.zeros((b, t, c), F32).at[:, 0].set(state[:, 0]).at[:, 1].set(state[:, 1])
    return p1.reshape(1, b * t, c), p2.reshape(1, b * t, c)


def kernel(x_prompt, x_sample, state_conv_b, state_ffn, cache_k, cache_v, cache_logf,
           norm_mix, norm_ffn, w_in_ab, sgu_norm, w_spatial, b_spatial, conv_b, w_out_ab,
           w_in_c, b_forget, q_norm, k_norm, w_out_c, w_up, conv_ffn, w_down):
    bp, tp, d = x_prompt.shape
    bs, ts, _ = x_sample.shape
    ns = bs * ts
    xs = x_sample.reshape(1, ns, d)

    pos = jnp.arange(GMLP_CHUNK)
    vis = (pos[None, :] // CHUNK) <= (pos[:, None] // CHUNK)
    w_m = jnp.where(vis[None], w_spatial[0], 0.0)
    wm_p = w_m.astype(BF16)
    bb_p = jnp.broadcast_to(b_spatial[0][:, :, None], (A_GROUPS, GMLP_CHUNK, A_GROUP_DIM))
    reps = GMLP_CHUNK // ts
    eye = jnp.eye(reps, dtype=F32)
    wm_s = jnp.einsum('ab,gts->gatbs', eye, w_m[:, :ts, :ts]).reshape(
        A_GROUPS, GMLP_CHUNK, GMLP_CHUNK).astype(BF16)
    bb_s = jnp.broadcast_to(jnp.tile(b_spatial[0][:, :ts], (1, reps))[:, :, None],
                            (A_GROUPS, GMLP_CHUNK, A_GROUP_DIM))
    g_mix0 = norm_mix[0].reshape(1, d)
    win_ab = w_in_ab[0].astype(BF16)
    sg = sgu_norm[0].reshape(1, D_A)
    cwb = conv_b[0]

    y_p, st_b_p = _mixer_ab(x_prompt, g_mix0, win_ab, sg, wm_p, bb_p, cwb)
    pb1, pb2 = _expand_state(state_conv_b[0], ts)
    y_s, c_s, v_s = _mixer_ab(xs, g_mix0, win_ab, sg, wm_s, bb_s, cwb, seg=ts, p1=pb1, p2=pb2)

    def ffn(layer, x_p, yy_p, x_s, yy_s, w_pre):
        g = norm_ffn[layer].reshape(1, d)
        wup = w_up[layer].astype(BF16)
        wdn = w_down[layer].astype(BF16)
        cw = conv_ffn[layer]
        wpre = w_pre.astype(BF16)
        xo_p, st_p = _post(x_p, yy_p, wpre, g, wup, cw, wdn)
        xo_s, st0_s, st1_s = _post(x_s, yy_s, wpre, g, wup, cw, wdn, seg=ts,
                                   s0=state_ffn[layer, :, 0], s1=state_ffn[layer, :, 1])
        st_s = jnp.stack([st0_s, st1_s], axis=1)
        return xo_p, st_p[:, SUBLANES - (CONV_W - 1):], xo_s, st_s

    xp1, ffn_p0, xs1, ffn_s0 = ffn(0, x_prompt, y_p, xs, y_s, w_out_ab[0])

    g_mix1 = norm_mix[1].reshape(1, d)
    wqkv = w_in_c[0][:, :3 * D_C].astype(BF16)
    wft = w_in_c[0][:, 3 * D_C:].T.astype(BF16)
    bf = b_forget[0].reshape(C_HEADS, 1)
    scale = C_HEAD_DIM ** -0.5
    qg = (jnp.tile(q_norm[0], C_HEADS) * scale).reshape(1, D_C)
    kg = jnp.tile(k_norm[0], C_HEADS).reshape(1, D_C)
    hid = jnp.arange(D_C) // C_HEAD_DIM
    hm = ((hid[:, None] == hid[None, :]).astype(F32) / C_HEAD_DIM).astype(BF16)

    q_p, kf_p, vf_p, kb_p, vb_p, lf_p = _fox_proj(xp1.reshape(bp * tp, d), g_mix1, wqkv, wft, bf,
                                                  qg, kg, hm)
    q_s, kf_s, vf_s, kb_s, vb_s, lf_s = _fox_proj(xs1.reshape(ns, d), g_mix1, wqkv, wft, bf,
                                                  qg, kg, hm)

    lf_p3 = lf_p.reshape(C_HEADS, bp, tp).transpose(1, 0, 2)
    d_p = _cumsum_lanes(lf_p3.reshape(bp * C_HEADS, tp)).reshape(bp, C_HEADS, tp)
    o_p = _att_prompt(q_p.reshape(bp, tp, D_C), kb_p.reshape(bp, tp, D_C),
                      vb_p.reshape(bp, tp, D_C), d_p)

    p_len = cache_k.shape[2]
    lf_s3 = lf_s.reshape(C_HEADS, bs, ts).transpose(1, 0, 2)
    lf_s_pad = jnp.pad(lf_s3, ((0, 0), (0, 0), (0, LANES - ts))).reshape(bs * C_HEADS, LANES)
    dkn = _cumsum_lanes(lf_s_pad).reshape(bs, C_HEADS, LANES)
    lf_c = cache_logf[0].astype(F32).transpose(0, 2, 1).reshape(bs * C_HEADS, p_len)
    dkc = _cumsum_lanes(lf_c, sub_last=True).reshape(bs, C_HEADS, p_len)
    o_s = _att_sample(q_s.reshape(bs, ts, D_C), cache_k[0].reshape(bs, p_len, D_C),
                      cache_v[0].reshape(bs, p_len, D_C), kb_s.reshape(bs, ts, D_C),
                      vb_s.reshape(bs, ts, D_C), dkc, dkn)

    xp2, ffn_p1, xs2, ffn_s1 = ffn(1, xp1, o_p, xs1, o_s.reshape(1, ns, D_C), w_out_c[0])

    heads = (C_HEADS, C_HEAD_DIM)
    return (xp2, xs2.reshape(bs, ts, d),
            st_b_p[None, :, SUBLANES - (CONV_W - 1):],
            c_s.reshape(bs, ts, D_B)[None, :, ts - (CONV_W - 1):],
            v_s.reshape(1, bs, ts, D_A),
            kf_p.reshape(1, bp, tp, *heads), vf_p.reshape(1, bp, tp, *heads),
            lf_p3.transpose(0, 2, 1)[None],
            kf_s.reshape(1, bs, ts, *heads), vf_s.reshape(1, bs, ts, *heads),
            lf_s3.transpose(0, 2, 1)[None],
            jnp.stack([ffn_p0, ffn_p1]), jnp.stack([ffn_s0, ffn_s1]))
```

```python
import functools

import jax
import jax.numpy as jnp
from jax import lax
from jax.experimental import pallas as pl
from jax.experimental.pallas import tpu as pltpu

F32 = jnp.float32
BF16 = jnp.bfloat16

D_MODEL = 1024
CHUNK = 64
GMLP_CHUNK = 128
A_GROUP_DIM = 128
D_A = D_MODEL // 2
A_GROUPS = D_A // A_GROUP_DIM
D_B = D_MODEL // 2
CONV_W = 3
C_HEAD_DIM = 64
C_HEADS = D_MODEL // C_HEAD_DIM
D_C = C_HEADS * C_HEAD_DIM
D_FF = 11 * D_MODEL // 4
EPS = 1e-6
NEG_INF = -1e30

LANES = 128
SUBLANES = 8
MXU_COLS = 256
VMEM_LIMIT_BYTES = 56 * 1024 * 1024
ROW_TILE = 512
FF_CHUNK = MXU_COLS
ATT_TQ = 1024
ATT_HEADS = 4
ATT_TK = MXU_COLS
LOG2E = 1.4426950408889634


def _params(sem):
    return pltpu.CompilerParams(dimension_semantics=sem, vmem_limit_bytes=VMEM_LIMIT_BYTES)


def _resident(shape):
    nd = len(shape)
    return pl.BlockSpec(shape, lambda *_: (0,) * nd, pipeline_mode=pl.Buffered(1))


def _rms(x, g):
    ms = jnp.mean(x * x, axis=-1, keepdims=True)
    return x * lax.rsqrt(ms + EPS) * g


def _shift_rows_carry(z, carry):
    r1 = pltpu.roll(z, 1, 0)
    r2 = pltpu.roll(z, 2, 0)
    rid = lax.broadcasted_iota(jnp.int32, (SUBLANES, z.shape[1]), 0)
    h1 = jnp.where(rid < 1, pltpu.roll(carry, 1, 0), r1[0:SUBLANES])
    h2 = jnp.where(rid < 2, pltpu.roll(carry, 2, 0), r2[0:SUBLANES])
    z1 = jnp.concatenate([h1, r1[SUBLANES:]], axis=0)
    z2 = jnp.concatenate([h2, r2[SUBLANES:]], axis=0)
    return z1, z2


def _shift_rows_segments(z, p1, p2, seg):
    rid = lax.broadcasted_iota(jnp.int32, z.shape, 0) & (seg - 1)
    z1 = jnp.where(rid >= 1, pltpu.roll(z, 1, 0), p1)
    z2 = jnp.where(rid >= 2, pltpu.roll(z, 2, 0), p2)
    return z1, z2


def _conv3(z, z1, z2, w):
    return z2 * w[0:1] + z1 * w[1:2] + z * w[2:3]


def _ab_kernel(*refs, tm, seg):
    if seg is None:
        (x_ref, g_ref, win_ref, sg_ref, wm_ref, bb_ref, cw_ref,
         y_ref, st_ref, carry_ref) = refs
    else:
        (x_ref, g_ref, win_ref, sg_ref, wm_ref, bb_ref, cw_ref, p1_ref, p2_ref,
         y_ref, c_ref, v_ref) = refs

    h = _rms(x_ref[...], g_ref[...]).astype(BF16)
    z = jnp.dot(h, win_ref[...], preferred_element_type=F32)

    ya = []
    vn = []
    for g in range(A_GROUPS):
        lo = g * A_GROUP_DIM
        u = jax.nn.gelu(z[:, lo:lo + A_GROUP_DIM])
        vg = jax.nn.gelu(z[:, D_A + lo:D_A + lo + A_GROUP_DIM])
        ms = jnp.mean(vg * vg, axis=-1, keepdims=True)
        vg = vg * lax.rsqrt(ms + EPS) * sg_ref[:, lo:lo + A_GROUP_DIM]
        vn.append(vg)
        vb = vg.astype(BF16)
        wm = wm_ref[g]
        bias = bb_ref[g]
        blocks = []
        for c in range(tm // GMLP_CHUNK):
            blk = vb[c * GMLP_CHUNK:(c + 1) * GMLP_CHUNK]
            blocks.append(jnp.dot(wm, blk, preferred_element_type=F32) + bias)
        ya.append(u * jnp.concatenate(blocks, axis=0))

    off = 2 * D_A
    g_b = z[:, off:off + D_B]
    c = z[:, off + D_B:off + 2 * D_B] * z[:, off + 2 * D_B:off + 3 * D_B]
    if seg is None:
        @pl.when(pl.program_id(1) == 0)
        def _():
            carry_ref[...] = jnp.zeros_like(carry_ref)
        c1, c2 = _shift_rows_carry(c, carry_ref[...])
        carry_ref[...] = c[tm - SUBLANES:tm]
        st_ref[...] = c[tm - SUBLANES:tm]
    else:
        c1, c2 = _shift_rows_segments(c, p1_ref[...], p2_ref[...], seg)
        c_ref[...] = c
        v_ref[...] = jnp.concatenate(vn, axis=1)
    yb = g_b * _conv3(c, c1, c2, cw_ref[...])
    y_ref[...] = jnp.concatenate(ya + [yb], axis=1).astype(BF16)


def _mixer_ab(x, g, win, sg, wm, bb, cw, seg=None, p1=None, p2=None):
    b, t, d = x.shape
    tm = min(ROW_TILE, t)
    nt = t // tm
    row = lambda bi, ti: (bi, ti, 0)
    in_specs = [pl.BlockSpec((None, tm, d), row), _resident(g.shape), _resident(win.shape),
                _resident(sg.shape), _resident(wm.shape), _resident(bb.shape), _resident(cw.shape)]
    args = [x, g, win, sg, wm, bb, cw]
    y_shape = jax.ShapeDtypeStruct((b, t, D_A + D_B), BF16)
    y_spec = pl.BlockSpec((None, tm, D_A + D_B), row)
    if seg is None:
        out_shape = (y_shape, jax.ShapeDtypeStruct((b, SUBLANES, D_B), F32))
        out_specs = (y_spec, pl.BlockSpec((None, SUBLANES, D_B), lambda bi, ti: (bi, 0, 0)))
        scratch = [pltpu.VMEM((SUBLANES, D_B), F32)]
    else:
        in_specs += [pl.BlockSpec((None, tm, D_B), row), pl.BlockSpec((None, tm, D_B), row)]
        args += [p1, p2]
        out_shape = (y_shape, jax.ShapeDtypeStruct((b, t, D_B), F32),
                     jax.ShapeDtypeStruct((b, t, D_A), F32))
        out_specs = (y_spec, pl.BlockSpec((None, tm, D_B), row), pl.BlockSpec((None, tm, D_A), row))
        scratch = []
    return pl.pallas_call(
        functools.partial(_ab_kernel, tm=tm, seg=seg),
        grid=(b, nt), in_specs=in_specs, out_specs=out_specs, out_shape=out_shape,
        scratch_shapes=scratch, compiler_params=_params(("arbitrary", "arbitrary")),
        name="mixer_ab_seg" if seg else "mixer_ab",
    )(*args)


def _post_kernel(*refs, tm, seg):
    if seg is None:
        (x_ref, y_ref, wpre_ref, g_ref, wup_ref, cw_ref, wdn_ref,
         xo_ref, st_ref, a_ref, carry_ref) = refs
    else:
        (x_ref, y_ref, wpre_ref, g_ref, wup_ref, cw_ref, wdn_ref, s0_ref, s1_ref,
         xo_ref, st0_ref, st1_ref, a_ref, p1_ref, p2_ref, zs_ref) = refs
        nseg = tm // seg

    x1 = x_ref[...] + jnp.dot(y_ref[...], wpre_ref[...], preferred_element_type=F32)
    h = _rms(x1, g_ref[...]).astype(BF16)

    if seg is None:
        @pl.when(pl.program_id(1) == 0)
        def _():
            carry_ref[...] = jnp.zeros_like(carry_ref)
    else:
        p1_ref[...] = jnp.zeros_like(p1_ref)
        p2_ref[...] = jnp.zeros_like(p2_ref)

    def conv_cols(lo):
        z = jnp.dot(h, wup_ref[:, lo:lo + FF_CHUNK], preferred_element_type=F32)
        if seg is None:
            z1, z2 = _shift_rows_carry(z, carry_ref[:, lo:lo + FF_CHUNK])
            carry_ref[:, lo:lo + FF_CHUNK] = z[tm - SUBLANES:tm]
            st_ref[:, lo:lo + FF_CHUNK] = z[tm - SUBLANES:tm]
        else:
            for i in range(FF_CHUNK // LANES):
                cols = slice(lo + i * LANES, lo + (i + 1) * LANES)
                p1_ref[i, pl.ds(0, nseg, stride=seg), :] = s1_ref[:, cols]
                p2_ref[i, pl.ds(0, nseg, stride=seg), :] = s0_ref[:, cols]
                p2_ref[i, pl.ds(1, nseg, stride=seg), :] = s1_ref[:, cols]
                zs_ref[i] = z[:, i * LANES:(i + 1) * LANES]
                st0_ref[:, cols] = zs_ref[i, pl.ds(seg - 2, nseg, stride=seg), :]
                st1_ref[:, cols] = zs_ref[i, pl.ds(seg - 1, nseg, stride=seg), :]
            p1 = jnp.concatenate([p1_ref[i] for i in range(FF_CHUNK // LANES)], axis=1)
            p2 = jnp.concatenate([p2_ref[i] for i in range(FF_CHUNK // LANES)], axis=1)
            z1, z2 = _shift_rows_segments(z, p1, p2, seg)
        return _conv3(z, z1, z2, cw_ref[:, lo:lo + FF_CHUNK])

    for j in range(D_FF // FF_CHUNK):
        gate = conv_cols(j * FF_CHUNK)
        up = conv_cols(D_FF + j * FF_CHUNK)
        a_ref[:, j * FF_CHUNK:(j + 1) * FF_CHUNK] = (jax.nn.silu(gate) * up).astype(BF16)

    xo_ref[...] = x1 + jnp.dot(a_ref[...], wdn_ref[...], preferred_element_type=F32)


def _post(x, y, wpre, g, wup, cw, wdn, seg=None, s0=None, s1=None):
    b, t, d = x.shape
    tm = min(ROW_TILE, t)
    nt = t // tm
    assert seg is None or (b == 1 and nt == 1), "packed streams must fit one row tile"
    row = lambda bi, ti: (bi, ti, 0)
    in_specs = [pl.BlockSpec((None, tm, d), row), pl.BlockSpec((None, tm, y.shape[2]), row),
                _resident(wpre.shape), _resident(g.shape), _resident(wup.shape),
                _resident(cw.shape), _resident(wdn.shape)]
    args = [x, y, wpre, g, wup, cw, wdn]
    xo_shape = jax.ShapeDtypeStruct((b, t, d), F32)
    xo_spec = pl.BlockSpec((None, tm, d), row)
    scratch = [pltpu.VMEM((tm, D_FF), BF16)]
    if seg is None:
        out_shape = (xo_shape, jax.ShapeDtypeStruct((b, SUBLANES, 2 * D_FF), F32))
        out_specs = (xo_spec, pl.BlockSpec((None, SUBLANES, 2 * D_FF), lambda bi, ti: (bi, 0, 0)))
        scratch.append(pltpu.VMEM((SUBLANES, 2 * D_FF), F32))
    else:
        in_specs += [_resident(s0.shape), _resident(s1.shape)]
        args += [s0, s1]
        st_shape = jax.ShapeDtypeStruct(s0.shape, F32)
        st_spec = pl.BlockSpec(s0.shape, lambda bi, ti: (0, 0))
        out_shape = (xo_shape, st_shape, st_shape)
        out_specs = (xo_spec, st_spec, st_spec)
        scratch += [pltpu.VMEM((FF_CHUNK // LANES, tm, LANES), F32)] * 3
    return pl.pallas_call(
        functools.partial(_post_kernel, tm=tm, seg=seg),
        grid=(b, nt), in_specs=in_specs, out_specs=out_specs, out_shape=out_shape,
        scratch_shapes=scratch, compiler_params=_params(("arbitrary", "arbitrary")),
        name="post_ffn_seg" if seg else "post_ffn",
    )(*args)


def _fox_proj_kernel(x_ref, g_ref, wqkv_ref, wf_ref, bf_ref, qg_ref, kg_ref, hm_ref,
                     q_ref, kf_ref, vf_ref, kb_ref, vb_ref, lf_ref):
    h = _rms(x_ref[...], g_ref[...]).astype(BF16)
    z = jnp.dot(h, wqkv_ref[...], preferred_element_type=F32)
    q = z[:, 0:D_C]
    k = z[:, D_C:2 * D_C]
    v = z[:, 2 * D_C:3 * D_C]
    hm = hm_ref[...]
    qms = jnp.dot((q * q).astype(BF16), hm, preferred_element_type=F32)
    kms = jnp.dot((k * k).astype(BF16), hm, preferred_element_type=F32)
    qn = q * lax.rsqrt(qms + EPS) * qg_ref[...]
    kn = k * lax.rsqrt(kms + EPS) * kg_ref[...]
    q_ref[...] = qn.astype(BF16)
    for hh in range(C_HEADS):
        kf_ref[:, hh, :] = kn[:, hh * C_HEAD_DIM:(hh + 1) * C_HEAD_DIM]
        vf_ref[:, hh, :] = v[:, hh * C_HEAD_DIM:(hh + 1) * C_HEAD_DIM]
    kb_ref[...] = kn.astype(BF16)
    vb_ref[...] = v.astype(BF16)
    f = lax.dot_general(wf_ref[...], h, (((1,), (1,)), ((), ())),
                        preferred_element_type=F32) + bf_ref[...]
    lf_ref[...] = jnp.minimum(f, 0.0) - jnp.log1p(jnp.exp(-jnp.abs(f)))


def _fox_proj(x2d, g, wqkv, wft, bf, qg, kg, hm):
    n, d = x2d.shape
    tm = min(ROW_TILE, n)
    row = lambda i: (i, 0)
    full = pl.BlockSpec((tm, D_C), row)
    split = pl.BlockSpec((tm, C_HEADS, C_HEAD_DIM), lambda i: (i, 0, 0))
    return pl.pallas_call(
        _fox_proj_kernel,
        grid=(n // tm,),
        in_specs=[pl.BlockSpec((tm, d), row), _resident(g.shape), _resident(wqkv.shape),
                  _resident(wft.shape), _resident(bf.shape), _resident(qg.shape),
                  _resident(kg.shape), _resident(hm.shape)],
        out_specs=(full, split, split, full, full, pl.BlockSpec((C_HEADS, tm), lambda i: (0, i))),
        out_shape=(jax.ShapeDtypeStruct((n, D_C), BF16),
                   jax.ShapeDtypeStruct((n, C_HEADS, C_HEAD_DIM), F32),
                   jax.ShapeDtypeStruct((n, C_HEADS, C_HEAD_DIM), F32),
                   jax.ShapeDtypeStruct((n, D_C), BF16),
                   jax.ShapeDtypeStruct((n, D_C), BF16), jax.ShapeDtypeStruct((C_HEADS, n), F32)),
        compiler_params=_params(("arbitrary",)),
        name="fox_proj",
    )(x2d, g, wqkv, wft, bf, qg, kg, hm)


def _cumsum_kernel(x_ref, tri_ref, o_ref, *, sub_last):
    r, n = x_ref.shape
    tri = tri_ref[...]
    off = jnp.zeros((r, 1), F32)
    for i in range(n // LANES):
        xb = x_ref[:, i * LANES:(i + 1) * LANES] * LOG2E
        hi = xb.astype(BF16)
        r1 = xb - hi.astype(F32)
        mid = r1.astype(BF16)
        lo = (r1 - mid.astype(F32)).astype(BF16)
        c = (jnp.dot(hi, tri, preferred_element_type=F32)
             + jnp.dot(mid, tri, preferred_element_type=F32)
             + jnp.dot(lo, tri, preferred_element_type=F32)) + off
        o_ref[:, i * LANES:(i + 1) * LANES] = c
        off = c[:, LANES - 1:LANES]
    if sub_last:
        o_ref[...] = o_ref[...] - off


def _cumsum_lanes(x, sub_last=False):
    tri = (jnp.arange(LANES)[:, None] <= jnp.arange(LANES)[None, :]).astype(BF16)
    return pl.pallas_call(
        functools.partial(_cumsum_kernel, sub_last=sub_last),
        out_shape=jax.ShapeDtypeStruct(x.shape, F32),
        compiler_params=pltpu.CompilerParams(vmem_limit_bytes=VMEM_LIMIT_BYTES),
        name="cumsum_lanes",
    )(x, tri)


def _att_prompt_kernel(q_ref, k_ref, v_ref, dk_ref, o_ref, v0_ref, v1_ref):
    hq = pl.program_id(1)
    qi = pl.program_id(2)
    tq = q_ref.shape[0]
    width = q_ref.shape[1]
    pairs = width // LANES
    low = (lax.broadcasted_iota(jnp.int32, (1, width), 1) & (LANES - 1)) < C_HEAD_DIM
    low1 = low[:, 0:LANES]

    @pl.when(qi == 0)
    def _():
        v = v_ref[...]
        one = jnp.ones_like(v)
        v0_ref[...] = jnp.where(low, v, one)
        v1_ref[...] = jnp.where(low, one, v)

    q = q_ref[...]
    zero = jnp.zeros_like(q)
    qh = (jnp.where(low, q, zero), jnp.where(low, zero, q))
    vh = (v0_ref, v1_ref)
    nt = (((1,), (1,)), ((), ()))

    def block(j, carry, r0, diag):
        ks = pl.multiple_of(j * ATT_TK, ATT_TK)
        out = []
        for c in range(2 * pairs):
            pr, e = divmod(c, 2)
            cols = slice(pr * LANES, (pr + 1) * LANES)
            m_all, acc_all = carry[2 * c:2 * c + 2]
            m, acc = m_all[r0:], acc_all[r0:]
            s = lax.dot_general(qh[e][r0:, cols], k_ref[pl.ds(ks, ATT_TK), cols], nt,
                                preferred_element_type=F32)
            s = s - dk_ref[pl.ds(2 * pairs * hq + c, 1), pl.ds(ks, ATT_TK)]
            if diag is not None:
                ri = lax.broadcasted_iota(jnp.int32, s.shape, 0) + r0
                ci = lax.broadcasted_iota(jnp.int32, s.shape, 1) + diag * ATT_TK
                s = jnp.where(ci <= ri, s, NEG_INF)
            m_new = jnp.maximum(m, jnp.max(s, axis=-1, keepdims=True))
            alpha = jnp.exp2(m - m_new)
            p = jnp.exp2(s - m_new).astype(BF16)
            acc = alpha * acc + jnp.dot(p, vh[e][pl.ds(ks, ATT_TK), cols],
                                        preferred_element_type=F32)
            if r0:
                m_new = jnp.concatenate([m_all[:r0], m_new], axis=0)
                acc = jnp.concatenate([acc_all[:r0], acc], axis=0)
            out += [m_new, acc]
        return tuple(out)

    init = (jnp.full((tq, 1), NEG_INF, F32), jnp.zeros((tq, LANES), F32)) * (2 * pairs)
    per_tile = tq // ATT_TK
    carry = lax.fori_loop(0, qi * per_tile, lambda j, c: block(j, c, 0, None), init)
    for d in range(per_tile):
        carry = block(qi * per_tile + d, carry, d * ATT_TK, d)
    for pr in range(pairs):
        acc0, acc1 = carry[4 * pr + 1], carry[4 * pr + 3]
        o0 = acc0 * (1.0 / acc0[:, C_HEAD_DIM:C_HEAD_DIM + 1])
        o1 = acc1 * (1.0 / acc1[:, 0:1])
        o_ref[:, pr * LANES:(pr + 1) * LANES] = jnp.where(low1, o0, o1).astype(BF16)


def _att_prompt(q, k, v, dk):
    b, t, _ = q.shape
    tq = min(ATT_TQ, t)
    nq = t // tq
    width = ATT_HEADS * C_HEAD_DIM
    return pl.pallas_call(
        _att_prompt_kernel,
        grid=(b, C_HEADS // ATT_HEADS, nq),
        in_specs=[pl.BlockSpec((None, tq, width), lambda bi, hi, qi: (bi, qi, hi)),
                  pl.BlockSpec((None, t, width), lambda bi, hi, qi: (bi, 0, hi)),
                  pl.BlockSpec((None, t, width), lambda bi, hi, qi: (bi, 0, hi)),
                  pl.BlockSpec((None, C_HEADS, t), lambda bi, hi, qi: (bi, 0, 0))],
        out_specs=pl.BlockSpec((None, tq, width), lambda bi, hi, qi: (bi, qi, hi)),
        out_shape=jax.ShapeDtypeStruct((b, t, D_C), BF16),
        scratch_shapes=[pltpu.VMEM((t, width), BF16), pltpu.VMEM((t, width), BF16)],
        compiler_params=_params(("arbitrary", "arbitrary", "arbitrary")),
        name="att_prompt",
    )(q, k, v, dk)


def _att_sample_kernel(q_ref, kc_ref, vc_ref, kn_ref, vn_ref, dkc_ref, dkn_ref, o_ref, *, t_new):
    rows = C_HEADS * t_new
    q = q_ref[...]
    qt = jnp.concatenate([q] * C_HEADS, axis=0)
    rhead = lax.broadcasted_iota(jnp.int32, (rows, D_C), 0) // t_new
    lhead = lax.broadcasted_iota(jnp.int32, (rows, D_C), 1) // C_HEAD_DIM
    own = rhead == lhead
    qrows = jnp.where(own, qt, jnp.zeros_like(qt))

    def expand(d):
        return jnp.concatenate(
            [jnp.broadcast_to(d[hh:hh + 1], (t_new, d.shape[1])) for hh in range(C_HEADS)], axis=0)

    nt = (((1,), (1,)), ((), ()))
    kc = kc_ref[...].astype(BF16)
    s_c = lax.dot_general(qrows, kc, nt, preferred_element_type=F32) - expand(dkc_ref[...])
    s_n = lax.dot_general(qrows, kn_ref[...], nt, preferred_element_type=F32)
    s_n = s_n - expand(dkn_ref[...])[:, 0:t_new]
    tpos = lax.broadcasted_iota(jnp.int32, (rows, t_new), 0) & (t_new - 1)
    spos = lax.broadcasted_iota(jnp.int32, (rows, t_new), 1)
    s_n = jnp.where(spos <= tpos, s_n, NEG_INF)

    m = jnp.maximum(jnp.max(s_c, axis=-1, keepdims=True), jnp.max(s_n, axis=-1, keepdims=True))
    p_c = jnp.exp2(s_c - m)
    p_n = jnp.exp2(s_n - m)
    l = jnp.sum(p_c, axis=-1, keepdims=True) + jnp.sum(p_n, axis=-1, keepdims=True)
    o_full = (jnp.dot(p_c.astype(BF16), vc_ref[...].astype(BF16), preferred_element_type=F32)
              + jnp.dot(p_n.astype(BF16), vn_ref[...], preferred_element_type=F32)) / l
    o_full = jnp.where(own, o_full, 0.0)
    o = o_full[0:t_new]
    for hh in range(1, C_HEADS):
        o = o + o_full[hh * t_new:(hh + 1) * t_new]
    o_ref[...] = o.astype(BF16)


def _att_sample(q, kc, vc, kn, vn, dkc, dkn):
    b, t_new, _ = q.shape
    p_len = kc.shape[1]
    per = lambda bi: (bi, 0, 0)
    return pl.pallas_call(
        functools.partial(_att_sample_kernel, t_new=t_new),
        grid=(b,),
        in_specs=[pl.BlockSpec((None, t_new, D_C), per), pl.BlockSpec((None, p_len, D_C), per),
                  pl.BlockSpec((None, p_len, D_C), per), pl.BlockSpec((None, t_new, D_C), per),
                  pl.BlockSpec((None, t_new, D_C), per), pl.BlockSpec((None, C_HEADS, p_len), per),
                  pl.BlockSpec((None, C_HEADS, LANES), per)],
        out_specs=pl.BlockSpec((None, t_new, D_C), per),
        out_shape=jax.ShapeDtypeStruct((b, t_new, D_C), BF16),
        compiler_params=_params(("arbitrary",)),
        name="att_sample",
    )(q, kc, vc, kn, vn, dkc, dkn)


def _expand_state(state, t):
    b, _, c = state.shape
    p1 = jnp.zeros((b, t, c), F32).at[:, 0].set(state[:, 1])
    p2 = jnp.zeros((b, t, c), F32).at[:, 0].set(state[:, 0]).at[:, 1].set(state[:, 1])
    return p1.reshape(1, b * t, c), p2.reshape(1, b * t, c)


def kernel(x_prompt, x_sample, state_conv_b, state_ffn, cache_k, cache_v, cache_logf,
           norm_mix, norm_ffn, w_in_ab, sgu_norm, w_spatial, b_spatial, conv_b, w_out_ab,
           w_in_c, b_forget, q_norm, k_norm, w_out_c, w_up, conv_ffn, w_down):
    bp, tp, d = x_prompt.shape
    bs, ts, _ = x_sample.shape
    ns = bs * ts
    xs = x_sample.reshape(1, ns, d)

    pos = jnp.arange(GMLP_CHUNK)
    vis = (pos[None, :] // CHUNK) <= (pos[:, None] // CHUNK)
    w_m = jnp.where(vis[None], w_spatial[0], 0.0)
    wm_p = w_m.astype(BF16)
    bb_p = jnp.broadcast_to(b_spatial[0][:, :, None], (A_GROUPS, GMLP_CHUNK, A_GROUP_DIM))
    reps = GMLP_CHUNK // ts
    eye = jnp.eye(reps, dtype=F32)
    wm_s = jnp.einsum('ab,gts->gatbs', eye, w_m[:, :ts, :ts]).reshape(
        A_GROUPS, GMLP_CHUNK, GMLP_CHUNK).astype(BF16)
    bb_s = jnp.broadcast_to(jnp.tile(b_spatial[0][:, :ts], (1, reps))[:, :, None],
                            (A_GROUPS, GMLP_CHUNK, A_GROUP_DIM))
    g_mix0 = norm_mix[0].reshape(1, d)
    win_ab = w_in_ab[0].astype(BF16)
    sg = sgu_norm[0].reshape(1, D_A)
    cwb = conv_b[0]

    y_p, st_b_p = _mixer_ab(x_prompt, g_mix0, win_ab, sg, wm_p, bb_p, cwb)
    pb1, pb2 = _expand_state(state_conv_b[0], ts)
    y_s, c_s, v_s = _mixer_ab(xs, g_mix0, win_ab, sg, wm_s, bb_s, cwb, seg=ts, p1=pb1, p2=pb2)

    def ffn(layer, x_p, yy_p, x_s, yy_s, w_pre):
        g = norm_ffn[layer].reshape(1, d)
        wup = w_up[layer].astype(BF16)
        wdn = w_down[layer].astype(BF16)
        cw = conv_ffn[layer]
        wpre = w_pre.astype(BF16)
        xo_p, st_p = _post(x_p, yy_p, wpre, g, wup, cw, wdn)
        xo_s, st0_s, st1_s = _post(x_s, yy_s, wpre, g, wup, cw, wdn, seg=ts,
                                   s0=state_ffn[layer, :, 0], s1=state_ffn[layer, :, 1])
        st_s = jnp.stack([st0_s, st1_s], axis=1)
        return xo_p, st_p[:, SUBLANES - (CONV_W - 1):], xo_s, st_s

    xp1, ffn_p0, xs1, ffn_s0 = ffn(0, x_prompt, y_p, xs, y_s, w_out_ab[0])

    g_mix1 = norm_mix[1].reshape(1, d)
    wqkv = w_in_c[0][:, :3 * D_C].astype(BF16)
    wft = w_in_c[0][:, 3 * D_C:].T.astype(BF16)
    bf = b_forget[0].reshape(C_HEADS, 1)
    scale = C_HEAD_DIM ** -0.5 * LOG2E
    qg =(jnp.tile(q_norm[0], C_HEADS) * scale).reshape(1, D_C)
    kg = jnp.tile(k_norm[0], C_HEADS).reshape(1, D_C)
    hid = jnp.arange(D_C) // C_HEAD_DIM
    hm = ((hid[:, None] == hid[None, :]).astype(F32) / C_HEAD_DIM).astype(BF16)

    q_p, kf_p, vf_p, kb_p, vb_p, lf_p = _fox_proj(xp1.reshape(bp * tp, d), g_mix1, wqkv, wft, bf,
                                                  qg, kg, hm)
    q_s, kf_s, vf_s, kb_s, vb_s, lf_s = _fox_proj(xs1.reshape(ns, d), g_mix1, wqkv, wft, bf,
                                                  qg, kg, hm)

    lf_p3 = lf_p.reshape(C_HEADS, bp, tp).transpose(1, 0, 2)
    d_p = _cumsum_lanes(lf_p3.reshape(bp * C_HEADS, tp)).reshape(bp, C_HEADS, tp)
    o_p = _att_prompt(q_p.reshape(bp, tp, D_C), kb_p.reshape(bp, tp, D_C),
                      vb_p.reshape(bp, tp, D_C), d_p)

    p_len = cache_k.shape[2]
    lf_s3 = lf_s.reshape(C_HEADS, bs, ts).transpose(1, 0, 2)
    lf_s_pad = jnp.pad(lf_s3, ((0, 0), (0, 0), (0, LANES - ts))).reshape(bs * C_HEADS, LANES)
    dkn = _cumsum_lanes(lf_s_pad).reshape(bs, C_HEADS, LANES)
    lf_c = cache_logf[0].astype(F32).transpose(0, 2, 1).reshape(bs * C_HEADS, p_len)
    dkc = _cumsum_lanes(lf_c, sub_last=True).reshape(bs, C_HEADS, p_len)
    o_s = _att_sample(q_s.reshape(bs, ts, D_C), cache_k[0].reshape(bs, p_len, D_C),
                      cache_v[0].reshape(bs, p_len, D_C), kb_s.reshape(bs, ts, D_C),
                      vb_s.reshape(bs, ts, D_C), dkc, dkn)

    xp2, ffn_p1, xs2, ffn_s1 = ffn(1, xp1, o_p, xs1, o_s.reshape(1, ns, D_C), w_out_c[0])

    heads = (C_HEADS, C_HEAD_DIM)
    return (xp2, xs2.reshape(bs, ts, d),
            st_b_p[None, :, SUBLANES - (CONV_W - 1):],
            c_s.reshape(bs, ts, D_B)[None, :, ts - (CONV_W - 1):],
            v_s.reshape(1, bs, ts, D_A),
            kf_p.reshape(1, bp, tp, *heads), vf_p.reshape(1, bp, tp, *heads),
            lf_p3.transpose(0, 2, 1)[None],
            kf_s.reshape(1, bs, ts, *heads), vf_s.reshape(1, bs, ts, *heads),
            lf_s3.transpose(0, 2, 1)[None],
            jnp.stack([ffn_p0, ffn_p1]), jnp.stack([ffn_s0, ffn_s1]))
```

```python
import functools

import jax
import jax.numpy as jnp
from jax import lax
from jax.experimental import pallas as pl
from jax.experimental.pallas import tpu as pltpu

F32 = jnp.float32
BF16 = jnp.bfloat16

D_MODEL = 1024
CHUNK = 64
GMLP_CHUNK = 128
A_GROUP_DIM = 128
D_A = D_MODEL // 2
A_GROUPS = D_A // A_GROUP_DIM
D_B = D_MODEL // 2
CONV_W = 3
C_HEAD_DIM = 64
C_HEADS = D_MODEL // C_HEAD_DIM
D_C = C_HEADS * C_HEAD_DIM
D_FF = 11 * D_MODEL // 4
EPS = 1e-6
NEG_INF = -1e30

LANES = 128
SUBLANES = 8
MXU_COLS = 256
VMEM_LIMIT_BYTES = 56 * 1024 * 1024
ROW_TILE = 512
FF_CHUNK = MXU_COLS
ATT_TQ = 1024
ATT_HEADS = 4
ATT_TK = MXU_COLS
LOG2E = 1.4426950408889634


def _params(sem):
    return pltpu.CompilerParams(dimension_semantics=sem, vmem_limit_bytes=VMEM_LIMIT_BYTES)


def _resident(shape):
    nd = len(shape)
    return pl.BlockSpec(shape, lambda *_: (0,) * nd, pipeline_mode=pl.Buffered(1))


def _rms(x, g):
    ms = jnp.mean(x * x, axis=-1, keepdims=True)
    return x * lax.rsqrt(ms + EPS) * g


def _shift_rows_carry(z, carry):
    r1 = pltpu.roll(z, 1, 0)
    r2 = pltpu.roll(z, 2, 0)
    rid = lax.broadcasted_iota(jnp.int32, (SUBLANES, z.shape[1]), 0)
    h1 = jnp.where(rid < 1, pltpu.roll(carry, 1, 0), r1[0:SUBLANES])
    h2 = jnp.where(rid < 2, pltpu.roll(carry, 2, 0), r2[0:SUBLANES])
    z1 = jnp.concatenate([h1, r1[SUBLANES:]], axis=0)
    z2 = jnp.concatenate([h2, r2[SUBLANES:]], axis=0)
    return z1, z2


def _shift_rows_segments(z, p1, p2, seg):
    rid = lax.broadcasted_iota(jnp.int32, z.shape, 0) & (seg - 1)
    z1 = jnp.where(rid >= 1, pltpu.roll(z, 1, 0), p1)
    z2 = jnp.where(rid >= 2, pltpu.roll(z, 2, 0), p2)
    return z1, z2


def _conv3(z, z1, z2, w):
    return z2 * w[0:1] + z1 * w[1:2] + z * w[2:3]


def _ab_kernel(*refs, tm, seg):
    if seg is None:
        (x_ref, g_ref, win_ref, sg_ref, wm_ref, bb_ref, cw_ref,
         y_ref, st_ref, carry_ref) = refs
    else:
        (x_ref, g_ref, win_ref, sg_ref, wm_ref, bb_ref, cw_ref, p1_ref, p2_ref,
         y_ref, c_ref, v_ref) = refs

    h = _rms(x_ref[...], g_ref[...]).astype(BF16)
    z = jnp.dot(h, win_ref[...], preferred_element_type=F32)

    ya = []
    vn = []
    for g in range(A_GROUPS):
        lo = g * A_GROUP_DIM
        u = jax.nn.gelu(z[:, lo:lo + A_GROUP_DIM])
        vg = jax.nn.gelu(z[:, D_A + lo:D_A + lo + A_GROUP_DIM])
        ms = jnp.mean(vg * vg, axis=-1, keepdims=True)
        vg = vg * lax.rsqrt(ms + EPS) * sg_ref[:, lo:lo + A_GROUP_DIM]
        vn.append(vg)
        vb = vg.astype(BF16)
        wm = wm_ref[g]
        bias = bb_ref[g]
        blocks = []
        for c in range(tm // GMLP_CHUNK):
            blk = vb[c * GMLP_CHUNK:(c + 1) * GMLP_CHUNK]
            blocks.append(jnp.dot(wm, blk, preferred_element_type=F32) + bias)
        ya.append(u * jnp.concatenate(blocks, axis=0))

    off = 2 * D_A
    g_b = z[:, off:off + D_B]
    c = z[:, off + D_B:off + 2 * D_B] * z[:, off + 2 * D_B:off + 3 * D_B]
    if seg is None:
        @pl.when(pl.program_id(1) == 0)
        def _():
            carry_ref[...] = jnp.zeros_like(carry_ref)
        c1, c2 = _shift_rows_carry(c, carry_ref[...])
        carry_ref[...] = c[tm - SUBLANES:tm]
        st_ref[...] = c[tm - SUBLANES:tm]
    else:
        c1, c2 = _shift_rows_segments(c, p1_ref[...], p2_ref[...], seg)
        c_ref[...] = c
        v_ref[...] = jnp.concatenate(vn, axis=1)
    yb = g_b * _conv3(c, c1, c2, cw_ref[...])
    y_ref[...] = jnp.concatenate(ya + [yb], axis=1).astype(BF16)


def _mixer_ab(x, g, win, sg, wm, bb, cw, seg=None, p1=None, p2=None):
    b, t, d = x.shape
    tm = min(ROW_TILE, t)
    nt = t // tm
    row = lambda bi, ti: (bi, ti, 0)
    in_specs = [pl.BlockSpec((None, tm, d), row), _resident(g.shape), _resident(win.shape),
                _resident(sg.shape), _resident(wm.shape), _resident(bb.shape), _resident(cw.shape)]
    args = [x, g, win, sg, wm, bb, cw]
    y_shape = jax.ShapeDtypeStruct((b, t, D_A + D_B), BF16)
    y_spec = pl.BlockSpec((None, tm, D_A + D_B), row)
    if seg is None:
        out_shape = (y_shape, jax.ShapeDtypeStruct((b, SUBLANES, D_B), F32))
        out_specs = (y_spec, pl.BlockSpec((None, SUBLANES, D_B), lambda bi, ti: (bi, 0, 0)))
        scratch = [pltpu.VMEM((SUBLANES, D_B), F32)]
    else:
        in_specs += [pl.BlockSpec((None, tm, D_B), row), pl.BlockSpec((None, tm, D_B), row)]
        args += [p1, p2]
        out_shape = (y_shape, jax.ShapeDtypeStruct((b, t, D_B), F32),
                     jax.ShapeDtypeStruct((b, t, D_A), F32))
        out_specs = (y_spec, pl.BlockSpec((None, tm, D_B), row), pl.BlockSpec((None, tm, D_A), row))
        scratch = []
    return pl.pallas_call(
        functools.partial(_ab_kernel, tm=tm, seg=seg),
        grid=(b, nt), in_specs=in_specs, out_specs=out_specs, out_shape=out_shape,
        scratch_shapes=scratch, compiler_params=_params(("arbitrary", "arbitrary")),
        name="mixer_ab_seg" if seg else "mixer_ab",
    )(*args)


def _post_kernel(*refs, tm, seg):
    if seg is None:
        (x_ref, y_ref, wpre_ref, g_ref, wup_ref, cw_ref, wdn_ref,
         xo_ref, st_ref, a_ref, carry_ref) = refs
    else:
        (x_ref, y_ref, wpre_ref, g_ref, wup_ref, cw_ref, wdn_ref, s0_ref, s1_ref,
         xo_ref, st0_ref, st1_ref, a_ref, p1_ref, p2_ref, zs_ref) = refs
        nseg = tm // seg

    x1 = x_ref[...] + jnp.dot(y_ref[...], wpre_ref[...], preferred_element_type=F32)
    h = _rms(x1, g_ref[...]).astype(BF16)

    if seg is None:
        @pl.when(pl.program_id(1) == 0)
        def _():
            carry_ref[...] = jnp.zeros_like(carry_ref)
    else:
        p1_ref[...] = jnp.zeros_like(p1_ref)
        p2_ref[...] = jnp.zeros_like(p2_ref)

    def conv_cols(lo):
        z = jnp.dot(h, wup_ref[:, lo:lo + FF_CHUNK], preferred_element_type=F32)
        if seg is None:
            z1, z2 = _shift_rows_carry(z, carry_ref[:, lo:lo + FF_CHUNK])
            carry_ref[:, lo:lo + FF_CHUNK] = z[tm - SUBLANES:tm]
            st_ref[:, lo:lo + FF_CHUNK] = z[tm - SUBLANES:tm]
        else:
            for i in range(FF_CHUNK // LANES):
                cols = slice(lo + i * LANES, lo + (i + 1) * LANES)
                p1_ref[i, pl.ds(0, nseg, stride=seg), :] = s1_ref[:, cols]
                p2_ref[i, pl.ds(0, nseg, stride=seg), :] = s0_ref[:, cols]
                p2_ref[i, pl.ds(1, nseg, stride=seg), :] = s1_ref[:, cols]
                zs_ref[i] = z[:, i * LANES:(i + 1) * LANES]
                st0_ref[:, cols] = zs_ref[i, pl.ds(seg - 2, nseg, stride=seg), :]
                st1_ref[:, cols] = zs_ref[i, pl.ds(seg - 1, nseg, stride=seg), :]
            p1 = jnp.concatenate([p1_ref[i] for i in range(FF_CHUNK // LANES)], axis=1)
            p2 = jnp.concatenate([p2_ref[i] for i in range(FF_CHUNK // LANES)], axis=1)
            z1, z2 = _shift_rows_segments(z, p1, p2, seg)
        return _conv3(z, z1, z2, cw_ref[:, lo:lo + FF_CHUNK])

    for j in range(D_FF // FF_CHUNK):
        gate = conv_cols(j * FF_CHUNK)
        up = conv_cols(D_FF + j * FF_CHUNK)
        a_ref[:, j * FF_CHUNK:(j + 1) * FF_CHUNK] = (jax.nn.silu(gate) * up).astype(BF16)

    xo_ref[...] = x1 + jnp.dot(a_ref[...], wdn_ref[...], preferred_element_type=F32)


def _post(x, y, wpre, g, wup, cw, wdn, seg=None, s0=None, s1=None):
    b, t, d = x.shape
    tm = min(ROW_TILE, t)
    nt = t // tm
    assert seg is None or (b == 1 and nt == 1), "packed streams must fit one row tile"
    row = lambda bi, ti: (bi, ti, 0)
    in_specs = [pl.BlockSpec((None, tm, d), row), pl.BlockSpec((None, tm, y.shape[2]), row),
                _resident(wpre.shape), _resident(g.shape), _resident(wup.shape),
                _resident(cw.shape), _resident(wdn.shape)]
    args = [x, y, wpre, g, wup, cw, wdn]
    xo_shape = jax.ShapeDtypeStruct((b, t, d), F32)
    xo_spec = pl.BlockSpec((None, tm, d), row)
    scratch = [pltpu.VMEM((tm, D_FF), BF16)]
    if seg is None:
        out_shape = (xo_shape, jax.ShapeDtypeStruct((b, SUBLANES, 2 * D_FF), F32))
        out_specs = (xo_spec, pl.BlockSpec((None, SUBLANES, 2 * D_FF), lambda bi, ti: (bi, 0, 0)))
        scratch.append(pltpu.VMEM((SUBLANES, 2 * D_FF), F32))
    else:
        in_specs += [_resident(s0.shape), _resident(s1.shape)]
        args += [s0, s1]
        st_shape = jax.ShapeDtypeStruct(s0.shape, F32)
        st_spec = pl.BlockSpec(s0.shape, lambda bi, ti: (0, 0))
        out_shape = (xo_shape, st_shape, st_shape)
        out_specs = (xo_spec, st_spec, st_spec)
        scratch += [pltpu.VMEM((FF_CHUNK // LANES, tm, LANES), F32)] * 3
    return pl.pallas_call(
        functools.partial(_post_kernel, tm=tm, seg=seg),
        grid=(b, nt), in_specs=in_specs, out_specs=out_specs, out_shape=out_shape,
        scratch_shapes=scratch, compiler_params=_params(("arbitrary", "arbitrary")),
        name="post_ffn_seg" if seg else "post_ffn",
    )(*args)


def _fox_proj_kernel(x_ref, g_ref, wqkv_ref, wf_ref, bf_ref, qg_ref, kg_ref, hm_ref,
                     q_ref, kf_ref, vf_ref, kb_ref, vb_ref, lf_ref):
    h = _rms(x_ref[...], g_ref[...]).astype(BF16)
    z = jnp.dot(h, wqkv_ref[...], preferred_element_type=F32)
    q = z[:, 0:D_C]
    k = z[:, D_C:2 * D_C]
    v = z[:, 2 * D_C:3 * D_C]
    hm = hm_ref[...]
    qms = jnp.dot((q * q).astype(BF16), hm, preferred_element_type=F32)
    kms = jnp.dot((k * k).astype(BF16), hm, preferred_element_type=F32)
    qn = q * lax.rsqrt(qms + EPS) * qg_ref[...]
    kn = k * lax.rsqrt(kms + EPS) * kg_ref[...]
    q_ref[...] = qn.astype(BF16)
    tm = q.shape[0]
    for hh in range(C_HEADS):
        kf_ref[pl.ds(hh, tm, stride=C_HEADS), :] = kn[:, hh * C_HEAD_DIM:(hh + 1) * C_HEAD_DIM]
        vf_ref[pl.ds(hh, tm, stride=C_HEADS), :] = v[:, hh * C_HEAD_DIM:(hh + 1) * C_HEAD_DIM]
    kb_ref[...] = kn.astype(BF16)
    vb_ref[...] = v.astype(BF16)
    f = lax.dot_general(wf_ref[...], h, (((1,), (1,)), ((), ())),
                        preferred_element_type=F32) + bf_ref[...]
    lf_ref[...] = jnp.minimum(f, 0.0) - jnp.log1p(jnp.exp(-jnp.abs(f)))


def _fox_proj(x2d, g, wqkv, wft, bf, qg, kg, hm):
    n, d = x2d.shape
    tm = min(ROW_TILE, n)
    row = lambda i: (i, 0)
    full = pl.BlockSpec((tm, D_C), row)
    split = pl.BlockSpec((tm * C_HEADS, C_HEAD_DIM), row)
    return pl.pallas_call(
        _fox_proj_kernel,
        grid=(n // tm,),
        in_specs=[pl.BlockSpec((tm, d), row), _resident(g.shape), _resident(wqkv.shape),
                  _resident(wft.shape), _resident(bf.shape), _resident(qg.shape),
                  _resident(kg.shape), _resident(hm.shape)],
        out_specs=(full, split, split, full, full, pl.BlockSpec((C_HEADS, tm), lambda i: (0, i))),
        out_shape=(jax.ShapeDtypeStruct((n, D_C), BF16),
                   jax.ShapeDtypeStruct((n * C_HEADS, C_HEAD_DIM), F32),
                   jax.ShapeDtypeStruct((n * C_HEADS, C_HEAD_DIM), F32),
                   jax.ShapeDtypeStruct((n, D_C), BF16),
                   jax.ShapeDtypeStruct((n, D_C), BF16), jax.ShapeDtypeStruct((C_HEADS, n), F32)),
        compiler_params=_params(("arbitrary",)),
        name="fox_proj",
    )(x2d, g, wqkv, wft, bf, qg, kg, hm)


def _cumsum_kernel(x_ref, tri_ref, o_ref, *, sub_last):
    r, n = x_ref.shape
    tri = tri_ref[...]
    off = jnp.zeros((r, 1), F32)
    for i in range(n // LANES):
        xb = x_ref[:, i * LANES:(i + 1) * LANES] * LOG2E
        hi = xb.astype(BF16)
        r1 = xb - hi.astype(F32)
        mid = r1.astype(BF16)
        lo = (r1 - mid.astype(F32)).astype(BF16)
        c = (jnp.dot(hi, tri, preferred_element_type=F32)
             + jnp.dot(mid, tri, preferred_element_type=F32)
             + jnp.dot(lo, tri, preferred_element_type=F32)) + off
        o_ref[:, i * LANES:(i + 1) * LANES] = c
        off = c[:, LANES - 1:LANES]
    if sub_last:
        o_ref[...] = o_ref[...] - off


def _cumsum_lanes(x, sub_last=False):
    tri = (jnp.arange(LANES)[:, None] <= jnp.arange(LANES)[None, :]).astype(BF16)
    return pl.pallas_call(
        functools.partial(_cumsum_kernel, sub_last=sub_last),
        out_shape=jax.ShapeDtypeStruct(x.shape, F32),
        compiler_params=pltpu.CompilerParams(vmem_limit_bytes=VMEM_LIMIT_BYTES),
        name="cumsum_lanes",
    )(x, tri)


def _att_prompt_kernel(q_ref, k_ref, v_ref, dk_ref, tri_ref, o_ref,
                       v0_ref, v1_ref, q0_ref, q1_ref, m_ref, acc_ref):
    hq = pl.program_id(1)
    qi = pl.program_id(2)
    tq = q_ref.shape[0]
    width = q_ref.shape[1]
    pairs = width // LANES
    low = (lax.broadcasted_iota(jnp.int32, (1, width), 1) & (LANES - 1)) < C_HEAD_DIM
    low1 = low[:, 0:LANES]

    @pl.when(qi == 0)
    def _():
        v = v_ref[...]
        one = jnp.ones_like(v)
        v0_ref[...] = jnp.where(low, v, one)
        v1_ref[...] = jnp.where(low, one, v)

    q = q_ref[...]
    zero = jnp.zeros_like(q)
    q0_ref[...] = jnp.where(low, q, zero)
    q1_ref[...] = jnp.where(low, zero, q)
    m_ref[...] = jnp.full(m_ref.shape, NEG_INF, F32)
    acc_ref[...] = jnp.zeros_like(acc_ref)
    qh = (q0_ref, q1_ref)
    vh = (v0_ref, v1_ref)
    nt = (((1,), (1,)), ((), ()))

    def update(c, ks, nk, r0, r1, causal):
        pr, e = divmod(c, 2)
        cols = slice(pr * LANES, (pr + 1) * LANES)
        span = nk * ATT_TK
        s = lax.dot_general(qh[e][r0:r1, cols], k_ref[pl.ds(ks, span), cols], nt,
                            preferred_element_type=F32)
        s = s - dk_ref[pl.ds(2 * pairs * hq + c, 1), pl.ds(ks, span)]
        if causal:
            last = s[:, span - ATT_TK:] + tri_ref[...]
            s = last if nk == 1 else jnp.concatenate([s[:, :span - ATT_TK], last], axis=1)
        m = m_ref[c, r0:r1, :]
        m_new = jnp.maximum(m, jnp.max(s, axis=-1, keepdims=True))
        alpha = jnp.exp2(m - m_new)
        p = jnp.exp2(s - jnp.concatenate([m_new] * (span // LANES), axis=1)).astype(BF16)
        pv = jnp.dot(p, vh[e][pl.ds(ks, span), cols], preferred_element_type=F32)
        acc_ref[c, r0:r1, :] = alpha * acc_ref[c, r0:r1, :] + pv
        m_ref[c, r0:r1, :] = m_new

    per_tile = tq // ATT_TK

    def full_block(j, _):
        ks = pl.multiple_of(j * ATT_TK, ATT_TK)
        for c in range(2 * pairs):
            update(c, ks, 1, 0, tq, False)
        return 0

    lax.fori_loop(0, qi * per_tile, full_block, 0)
    ks0 = pl.multiple_of(qi * tq, ATT_TK)
    for i in range(per_tile):
        for c in range(2 * pairs):
            update(c, ks0, i + 1, i * ATT_TK, (i + 1) * ATT_TK, True)
    for pr in range(pairs):
        acc0, acc1 = acc_ref[2 * pr], acc_ref[2 * pr + 1]
        o0 = acc0 * (1.0 / acc0[:, C_HEAD_DIM:C_HEAD_DIM + 1])
        o1 = acc1 * (1.0 / acc1[:, 0:1])
        o_ref[:, pr * LANES:(pr + 1) * LANES] = jnp.where(low1, o0, o1).astype(BF16)


def _att_prompt(q, k, v, dk):
    b, t, _ = q.shape
    tq = min(ATT_TQ, t)
    nq = t // tq
    width = ATT_HEADS * C_HEAD_DIM
    pos = jnp.arange(ATT_TK)
    tri = jnp.where(pos[None, :] <= pos[:, None], 0.0, NEG_INF).astype(F32)
    return pl.pallas_call(
        _att_prompt_kernel,
        grid=(b, C_HEADS // ATT_HEADS, nq),
        in_specs=[pl.BlockSpec((None, tq, width), lambda bi, hi, qi: (bi, qi, hi)),
                  pl.BlockSpec((None, t, width), lambda bi, hi, qi: (bi, 0, hi)),
                  pl.BlockSpec((None, t, width), lambda bi, hi, qi: (bi, 0, hi)),
                  pl.BlockSpec((None, C_HEADS, t), lambda bi, hi, qi: (bi, 0, 0)),
                  _resident(tri.shape)],
        out_specs=pl.BlockSpec((None, tq, width), lambda bi, hi, qi: (bi, qi, hi)),
        out_shape=jax.ShapeDtypeStruct((b, t, D_C), BF16),
        scratch_shapes=[pltpu.VMEM((t, width), BF16), pltpu.VMEM((t, width), BF16),
                        pltpu.VMEM((tq, width), BF16), pltpu.VMEM((tq, width), BF16),
                        pltpu.VMEM((ATT_HEADS, tq, LANES), F32),
                        pltpu.VMEM((ATT_HEADS, tq, LANES), F32)],
        compiler_params=_params(("arbitrary", "arbitrary", "arbitrary")),
        name="att_prompt",
    )(q, k, v, dk, tri)


def _att_sample_kernel(q_ref, kc_ref, vc_ref, kn_ref, vn_ref, dkc_ref, dkn_ref, o_ref, *, t_new):
    rows = C_HEADS * t_new
    q = q_ref[...]
    qt = jnp.concatenate([q] * C_HEADS, axis=0)
    rhead = lax.broadcasted_iota(jnp.int32, (rows, D_C), 0) // t_new
    lhead = lax.broadcasted_iota(jnp.int32, (rows, D_C), 1) // C_HEAD_DIM
    own = rhead == lhead
    qrows = jnp.where(own, qt, jnp.zeros_like(qt))

    def expand(d):
        return jnp.concatenate(
            [jnp.broadcast_to(d[hh:hh + 1], (t_new, d.shape[1])) for hh in range(C_HEADS)], axis=0)

    nt = (((1,), (1,)), ((), ()))
    kc = kc_ref[...].astype(BF16)
    s_c = lax.dot_general(qrows, kc, nt, preferred_element_type=F32) - expand(dkc_ref[...])
    s_n = lax.dot_general(qrows, kn_ref[...], nt, preferred_element_type=F32)
    s_n = s_n - expand(dkn_ref[...])[:, 0:t_new]
    tpos = lax.broadcasted_iota(jnp.int32, (rows, t_new), 0) & (t_new - 1)
    spos = lax.broadcasted_iota(jnp.int32, (rows, t_new), 1)
    s_n = jnp.where(spos <= tpos, s_n, NEG_INF)

    m = jnp.maximum(jnp.max(s_c, axis=-1, keepdims=True), jnp.max(s_n, axis=-1, keepdims=True))
    p_c = jnp.exp2(s_c - m)
    p_n = jnp.exp2(s_n - m)
    l = jnp.sum(p_c, axis=-1, keepdims=True) + jnp.sum(p_n, axis=-1, keepdims=True)
    o_full = (jnp.dot(p_c.astype(BF16), vc_ref[...].astype(BF16), preferred_element_type=F32)
              + jnp.dot(p_n.astype(BF16), vn_ref[...], preferred_element_type=F32)) / l
    o_full = jnp.where(own, o_full, 0.0)
    o = o_full[0:t_new]
    for hh in range(1, C_HEADS):
        o = o + o_full[hh * t_new:(hh + 1) * t_new]
    o_ref[...] = o.astype(BF16)


def _att_sample(q, kc, vc, kn, vn, dkc, dkn):
    b, t_new, _ = q.shape
    p_len = kc.shape[1]
    per = lambda bi: (bi, 0, 0)
    return pl.pallas_call(
        functools.partial(_att_sample_kernel, t_new=t_new),
        grid=(b,),
        in_specs=[pl.BlockSpec((None, t_new, D_C), per), pl.BlockSpec((None, p_len, D_C), per),
                  pl.BlockSpec((None, p_len, D_C), per), pl.BlockSpec((None, t_new, D_C), per),
                  pl.BlockSpec((None, t_new, D_C), per), pl.BlockSpec((None, C_HEADS, p_len), per),
                  pl.BlockSpec((None, C_HEADS, LANES), per)],
        out_specs=pl.BlockSpec((None, t_new, D_C), per),
        out_shape=jax.ShapeDtypeStruct((b, t_new, D_C), BF16),
        compiler_params=_params(("arbitrary",)),
        name="att_sample",
    )(q, kc, vc, kn, vn, dkc, dkn)


def _expand_state(state, t):
    b, _, c = state.shape
    p1 = jnp.zeros((b, t, c), F32).at[:, 0].set(state[:, 1])
    p2 = jnp.zeros((b, t, c), F32).at[:, 0].set(state[:, 0]).at[:, 1].set(state[:, 1])
    return p1.reshape(1, b * t, c), p2.reshape(1, b * t, c)


def kernel(x_prompt, x_sample, state_conv_b, state_ffn, cache_k, cache_v, cache_logf,
           norm_mix, norm_ffn, w_in_ab, sgu_norm, w_spatial, b_spatial, conv_b, w_out_ab,
           w_in_c, b_forget, q_norm, k_norm, w_out_c, w_up, conv_ffn, w_down):
    bp, tp, d = x_prompt.shape
    bs, ts, _ = x_sample.shape
    ns = bs * ts
    xs = x_sample.reshape(1, ns, d)

    pos = jnp.arange(GMLP_CHUNK)
    vis = (pos[None, :] // CHUNK) <= (pos[:, None] // CHUNK)
    w_m = jnp.where(vis[None], w_spatial[0], 0.0)
    wm_p = w_m.astype(BF16)
    bb_p = jnp.broadcast_to(b_spatial[0][:, :, None], (A_GROUPS, GMLP_CHUNK, A_GROUP_DIM))
    reps = GMLP_CHUNK // ts
    eye = jnp.eye(reps, dtype=F32)
    wm_s = jnp.einsum('ab,gts->gatbs', eye, w_m[:, :ts, :ts]).reshape(
        A_GROUPS, GMLP_CHUNK, GMLP_CHUNK).astype(BF16)
    bb_s = jnp.broadcast_to(jnp.tile(b_spatial[0][:, :ts], (1, reps))[:, :, None],
                            (A_GROUPS, GMLP_CHUNK, A_GROUP_DIM))
    g_mix0 = norm_mix[0].reshape(1, d)
    win_ab = w_in_ab[0].astype(BF16)
    sg = sgu_norm[0].reshape(1, D_A)
    cwb = conv_b[0]

    y_p, st_b_p = _mixer_ab(x_prompt, g_mix0, win_ab, sg, wm_p, bb_p, cwb)
    pb1, pb2 = _expand_state(state_conv_b[0], ts)
    y_s, c_s, v_s = _mixer_ab(xs, g_mix0, win_ab, sg, wm_s, bb_s, cwb, seg=ts, p1=pb1, p2=pb2)

    def ffn(layer, x_p, yy_p, x_s, yy_s, w_pre):
        g = norm_ffn[layer].reshape(1, d)
        wup = w_up[layer].astype(BF16)
        wdn = w_down[layer].astype(BF16)
        cw = conv_ffn[layer]
        wpre = w_pre.astype(BF16)
        xo_p, st_p = _post(x_p, yy_p, wpre, g, wup, cw, wdn)
        xo_s, st0_s, st1_s = _post(x_s, yy_s, wpre, g, wup, cw, wdn, seg=ts,
                                   s0=state_ffn[layer, :, 0], s1=state_ffn[layer, :, 1])
        st_s = jnp.stack([st0_s, st1_s], axis=1)
        return xo_p, st_p[:, SUBLANES - (CONV_W - 1):], xo_s, st_s

    xp1, ffn_p0, xs1, ffn_s0 = ffn(0, x_prompt, y_p, xs, y_s, w_out_ab[0])

    g_mix1 = norm_mix[1].reshape(1, d)
    wqkv = w_in_c[0][:, :3 * D_C].astype(BF16)
    wft = w_in_c[0][:, 3 * D_C:].T.astype(BF16)
    bf = b_forget[0].reshape(C_HEADS, 1)
    scale = C_HEAD_DIM ** -0.5 * LOG2E
    qg =(jnp.tile(q_norm[0], C_HEADS) * scale).reshape(1, D_C)
    kg = jnp.tile(k_norm[0], C_HEADS).reshape(1, D_C)
    hid = jnp.arange(D_C) // C_HEAD_DIM
    hm = ((hid[:, None] == hid[None, :]).astype(F32) / C_HEAD_DIM).astype(BF16)

    q_p, kf_p, vf_p, kb_p, vb_p, lf_p = _fox_proj(xp1.reshape(bp * tp, d), g_mix1, wqkv, wft, bf,
                                                  qg, kg, hm)
    q_s, kf_s, vf_s, kb_s, vb_s, lf_s = _fox_proj(xs1.reshape(ns, d), g_mix1, wqkv, wft, bf,
                                                  qg, kg, hm)

    lf_p3 = lf_p.reshape(C_HEADS, bp, tp).transpose(1, 0, 2)
    d_p = _cumsum_lanes(lf_p3.reshape(bp * C_HEADS, tp)).reshape(bp, C_HEADS, tp)
    o_p = _att_prompt(q_p.reshape(bp, tp, D_C), kb_p.reshape(bp, tp, D_C),
                      vb_p.reshape(bp, tp, D_C), d_p)

    p_len = cache_k.shape[2]
    lf_s3 = lf_s.reshape(C_HEADS, bs, ts).transpose(1, 0, 2)
    lf_s_pad = jnp.pad(lf_s3, ((0, 0), (0, 0), (0, LANES - ts))).reshape(bs * C_HEADS, LANES)
    dkn = _cumsum_lanes(lf_s_pad).reshape(bs, C_HEADS, LANES)
    lf_c = cache_logf[0].astype(F32).transpose(0, 2, 1).reshape(bs * C_HEADS, p_len)
    dkc = _cumsum_lanes(lf_c, sub_last=True).reshape(bs, C_HEADS, p_len)
    o_s = _att_sample(q_s.reshape(bs, ts, D_C), cache_k[0].reshape(bs, p_len, D_C),
                      cache_v[0].reshape(bs, p_len, D_C), kb_s.reshape(bs, ts, D_C),
                      vb_s.reshape(bs, ts, D_C), dkc, dkn)

    xp2, ffn_p1, xs2, ffn_s1 = ffn(1, xp1, o_p, xs1, o_s.reshape(1, ns, D_C), w_out_c[0])

    heads = (C_HEADS, C_HEAD_DIM)
    return (xp2, xs2.reshape(bs, ts, d),
            st_b_p[None, :, SUBLANES - (CONV_W - 1):],
            c_s.reshape(bs, ts, D_B)[None, :, ts - (CONV_W - 1):],
            v_s.reshape(1, bs, ts, D_A),
            kf_p.reshape(1, bp, tp, *heads), vf_p.reshape(1, bp, tp, *heads),
            lf_p3.transpose(0, 2, 1)[None],
            kf_s.reshape(1, bs, ts, *heads), vf_s.reshape(1, bs, ts, *heads),
            lf_s3.transpose(0, 2, 1)[None],
            jnp.stack([ffn_p0, ffn_p1]), jnp.stack([ffn_s0, ffn_s1]))
```

```python
import functools

import jax
import jax.numpy as jnp
from jax import lax
from jax.experimental import pallas as pl
from jax.experimental.pallas import tpu as pltpu

F32 = jnp.float32
BF16 = jnp.bfloat16

D_MODEL = 1024
CHUNK = 64
GMLP_CHUNK = 128
A_GROUP_DIM = 128
D_A = D_MODEL // 2
A_GROUPS = D_A // A_GROUP_DIM
D_B = D_MODEL // 2
CONV_W = 3
C_HEAD_DIM = 64
C_HEADS = D_MODEL // C_HEAD_DIM
D_C = C_HEADS * C_HEAD_DIM
D_FF = 11 * D_MODEL // 4
EPS = 1e-6
NEG_INF = -1e30

LANES = 128
SUBLANES = 8
MXU_COLS = 256
VMEM_LIMIT_BYTES = 56 * 1024 * 1024
ROW_TILE = 512
FF_CHUNK = MXU_COLS
ATT_TQ = 1024
ATT_HEADS = 4
ATT_TK = MXU_COLS
SAMPLE_TS = 512
LOG2E = 1.4426950408889634


def _params(sem):
    return pltpu.CompilerParams(dimension_semantics=sem, vmem_limit_bytes=VMEM_LIMIT_BYTES)


def _resident(shape):
    nd = len(shape)
    return pl.BlockSpec(shape, lambda *_: (0,) * nd, pipeline_mode=pl.Buffered(1))


def _rms(x, g):
    ms = jnp.mean(x * x, axis=-1, keepdims=True)
    return x * lax.rsqrt(ms + EPS) * g


def _shift_rows_carry(z, carry):
    r1 = pltpu.roll(z, 1, 0)
    r2 = pltpu.roll(z, 2, 0)
    rid = lax.broadcasted_iota(jnp.int32, (SUBLANES, z.shape[1]), 0)
    h1 = jnp.where(rid < 1, pltpu.roll(carry, 1, 0), r1[0:SUBLANES])
    h2 = jnp.where(rid < 2, pltpu.roll(carry, 2, 0), r2[0:SUBLANES])
    z1 = jnp.concatenate([h1, r1[SUBLANES:]], axis=0)
    z2 = jnp.concatenate([h2, r2[SUBLANES:]], axis=0)
    return z1, z2


def _shift_rows_segments(z, p1, p2, seg):
    rid = lax.broadcasted_iota(jnp.int32, z.shape, 0) & (seg - 1)
    z1 = jnp.where(rid >= 1, pltpu.roll(z, 1, 0), p1)
    z2 = jnp.where(rid >= 2, pltpu.roll(z, 2, 0), p2)
    return z1, z2


def _conv3(z, z1, z2, w):
    return z2 * w[0:1] + z1 * w[1:2] + z * w[2:3]


def _ab_kernel(*refs, tm, seg):
    if seg is None:
        (x_ref, g_ref, win_ref, sg_ref, wm_ref, bb_ref, cw_ref,
         y_ref, st_ref, carry_ref) = refs
    else:
        (x_ref, g_ref, win_ref, sg_ref, wm_ref, bb_ref, cw_ref, p1_ref, p2_ref,
         y_ref, c_ref, v_ref) = refs

    h = _rms(x_ref[...], g_ref[...]).astype(BF16)
    z = jnp.dot(h, win_ref[...], preferred_element_type=F32)

    ya = []
    vn = []
    for g in range(A_GROUPS):
        lo = g * A_GROUP_DIM
        u = jax.nn.gelu(z[:, lo:lo + A_GROUP_DIM])
        vg = jax.nn.gelu(z[:, D_A + lo:D_A + lo + A_GROUP_DIM])
        ms = jnp.mean(vg * vg, axis=-1, keepdims=True)
        vg = vg * lax.rsqrt(ms + EPS) * sg_ref[:, lo:lo + A_GROUP_DIM]
        vn.append(vg)
        vb = vg.astype(BF16)
        wm = wm_ref[g]
        bias = bb_ref[g]
        blocks = []
        for c in range(tm // GMLP_CHUNK):
            blk = vb[c * GMLP_CHUNK:(c + 1) * GMLP_CHUNK]
            blocks.append(jnp.dot(wm, blk, preferred_element_type=F32) + bias)
        ya.append(u * jnp.concatenate(blocks, axis=0))

    off = 2 * D_A
    g_b = z[:, off:off + D_B]
    c = z[:, off + D_B:off + 2 * D_B] * z[:, off + 2 * D_B:off + 3 * D_B]
    if seg is None:
        @pl.when(pl.program_id(1) == 0)
        def _():
            carry_ref[...] = jnp.zeros_like(carry_ref)
        c1, c2 = _shift_rows_carry(c, carry_ref[...])
        carry_ref[...] = c[tm - SUBLANES:tm]
        st_ref[...] = c[tm - SUBLANES:tm]
    else:
        c1, c2 = _shift_rows_segments(c, p1_ref[...], p2_ref[...], seg)
        c_ref[...] = c
        v_ref[...] = jnp.concatenate(vn, axis=1)
    yb = g_b * _conv3(c, c1, c2, cw_ref[...])
    y_ref[...] = jnp.concatenate(ya + [yb], axis=1).astype(BF16)


def _mixer_ab(x, g, win, sg, wm, bb, cw, seg=None, p1=None, p2=None):
    b, t, d = x.shape
    tm = min(ROW_TILE, t)
    nt = t // tm
    row = lambda bi, ti: (bi, ti, 0)
    in_specs = [pl.BlockSpec((None, tm, d), row), _resident(g.shape), _resident(win.shape),
                _resident(sg.shape), _resident(wm.shape), _resident(bb.shape), _resident(cw.shape)]
    args = [x, g, win, sg, wm, bb, cw]
    y_shape = jax.ShapeDtypeStruct((b, t, D_A + D_B), BF16)
    y_spec = pl.BlockSpec((None, tm, D_A + D_B), row)
    if seg is None:
        out_shape = (y_shape, jax.ShapeDtypeStruct((b, SUBLANES, D_B), F32))
        out_specs = (y_spec, pl.BlockSpec((None, SUBLANES, D_B), lambda bi, ti: (bi, 0, 0)))
        scratch = [pltpu.VMEM((SUBLANES, D_B), F32)]
    else:
        in_specs += [pl.BlockSpec((None, tm, D_B), row), pl.BlockSpec((None, tm, D_B), row)]
        args += [p1, p2]
        out_shape = (y_shape, jax.ShapeDtypeStruct((b, t, D_B), F32),
                     jax.ShapeDtypeStruct((b, t, D_A), F32))
        out_specs = (y_spec, pl.BlockSpec((None, tm, D_B), row), pl.BlockSpec((None, tm, D_A), row))
        scratch = []
    return pl.pallas_call(
        functools.partial(_ab_kernel, tm=tm, seg=seg),
        grid=(b, nt), in_specs=in_specs, out_specs=out_specs, out_shape=out_shape,
        scratch_shapes=scratch, compiler_params=_params(("arbitrary", "arbitrary")),
        name="mixer_ab_seg" if seg else "mixer_ab",
    )(*args)


def _post_kernel(*refs, tm, seg):
    if seg is None:
        (x_ref, y_ref, wpre_ref, g_ref, wup_ref, cw_ref, wdn_ref,
         xo_ref, st_ref, a_ref, carry_ref) = refs
    else:
        (x_ref, y_ref, wpre_ref, g_ref, wup_ref, cw_ref, wdn_ref, s0_ref, s1_ref,
         xo_ref, st0_ref, st1_ref, a_ref, p1_ref, p2_ref, zs_ref) = refs
        nseg = tm // seg

    x1 = x_ref[...] + jnp.dot(y_ref[...], wpre_ref[...], preferred_element_type=F32)
    h = _rms(x1, g_ref[...]).astype(BF16)

    if seg is None:
        @pl.when(pl.program_id(1) == 0)
        def _():
            carry_ref[...] = jnp.zeros_like(carry_ref)
    else:
        p1_ref[...] = jnp.zeros_like(p1_ref)
        p2_ref[...] = jnp.zeros_like(p2_ref)

    def conv_cols(lo):
        z = jnp.dot(h, wup_ref[:, lo:lo + FF_CHUNK], preferred_element_type=F32)
        if seg is None:
            z1, z2 = _shift_rows_carry(z, carry_ref[:, lo:lo + FF_CHUNK])
            carry_ref[:, lo:lo + FF_CHUNK] = z[tm - SUBLANES:tm]
            st_ref[:, lo:lo + FF_CHUNK] = z[tm - SUBLANES:tm]
        else:
            for i in range(FF_CHUNK // LANES):
                cols = slice(lo + i * LANES, lo + (i + 1) * LANES)
                p1_ref[i, pl.ds(0, nseg, stride=seg), :] = s1_ref[:, cols]
                p2_ref[i, pl.ds(0, nseg, stride=seg), :] = s0_ref[:, cols]
                p2_ref[i, pl.ds(1, nseg, stride=seg), :] = s1_ref[:, cols]
                zs_ref[i] = z[:, i * LANES:(i + 1) * LANES]
                st0_ref[:, cols] = zs_ref[i, pl.ds(seg - 2, nseg, stride=seg), :]
                st1_ref[:, cols] = zs_ref[i, pl.ds(seg - 1, nseg, stride=seg), :]
            p1 = jnp.concatenate([p1_ref[i] for i in range(FF_CHUNK // LANES)], axis=1)
            p2 = jnp.concatenate([p2_ref[i] for i in range(FF_CHUNK // LANES)], axis=1)
            z1, z2 = _shift_rows_segments(z, p1, p2, seg)
        return _conv3(z, z1, z2, cw_ref[:, lo:lo + FF_CHUNK])

    for j in range(D_FF // FF_CHUNK):
        gate = conv_cols(j * FF_CHUNK)
        up = conv_cols(D_FF + j * FF_CHUNK)
        a_ref[:, j * FF_CHUNK:(j + 1) * FF_CHUNK] = (jax.nn.silu(gate) * up).astype(BF16)

    xo_ref[...] = x1 + jnp.dot(a_ref[...], wdn_ref[...], preferred_element_type=F32)


def _post(x, y, wpre, g, wup, cw, wdn, seg=None, s0=None, s1=None):
    b, t, d = x.shape
    tm = min(ROW_TILE, t)
    nt = t // tm
    assert seg is None or (b == 1 and nt == 1), "packed streams must fit one row tile"
    row = lambda bi, ti: (bi, ti, 0)
    in_specs = [pl.BlockSpec((None, tm, d), row), pl.BlockSpec((None, tm, y.shape[2]), row),
                _resident(wpre.shape), _resident(g.shape), _resident(wup.shape),
                _resident(cw.shape), _resident(wdn.shape)]
    args = [x, y, wpre, g, wup, cw, wdn]
    xo_shape = jax.ShapeDtypeStruct((b, t, d), F32)
    xo_spec = pl.BlockSpec((None, tm, d), row)
    scratch = [pltpu.VMEM((tm, D_FF), BF16)]
    if seg is None:
        out_shape = (xo_shape, jax.ShapeDtypeStruct((b, SUBLANES, 2 * D_FF), F32))
        out_specs = (xo_spec, pl.BlockSpec((None, SUBLANES, 2 * D_FF), lambda bi, ti: (bi, 0, 0)))
        scratch.append(pltpu.VMEM((SUBLANES, 2 * D_FF), F32))
    else:
        in_specs += [_resident(s0.shape), _resident(s1.shape)]
        args += [s0, s1]
        st_shape = jax.ShapeDtypeStruct(s0.shape, F32)
        st_spec = pl.BlockSpec(s0.shape, lambda bi, ti: (0, 0))
        out_shape = (xo_shape, st_shape, st_shape)
        out_specs = (xo_spec, st_spec, st_spec)
        scratch += [pltpu.VMEM((FF_CHUNK // LANES, tm, LANES), F32)] * 3
    return pl.pallas_call(
        functools.partial(_post_kernel, tm=tm, seg=seg),
        grid=(b, nt), in_specs=in_specs, out_specs=out_specs, out_shape=out_shape,
        scratch_shapes=scratch, compiler_params=_params(("arbitrary", "arbitrary")),
        name="post_ffn_seg" if seg else "post_ffn",
    )(*args)


def _fox_proj_kernel(x_ref, g_ref, wqkv_ref, wf_ref, bf_ref, qg_ref, kg_ref,
                     q_ref, kf_ref, vf_ref, kb_ref, vb_ref, lf_ref):
    h = _rms(x_ref[...], g_ref[...]).astype(BF16)
    z = jnp.dot(h, wqkv_ref[...], preferred_element_type=F32)
    q = z[:, 0:D_C]
    k = z[:, D_C:2 * D_C]
    v = z[:, 2 * D_C:3 * D_C]
    low = lax.broadcasted_iota(jnp.int32, (1, LANES), 1) < C_HEAD_DIM

    def head_norm(x, gain):
        out = []
        for c in range(D_C // LANES):
            xb = x[:, c * LANES:(c + 1) * LANES]
            sq = xb * xb
            lo = jnp.sum(jnp.where(low, sq, 0.0), axis=-1, keepdims=True)
            hi = jnp.sum(jnp.where(low, 0.0, sq), axis=-1, keepdims=True)
            ms = jnp.where(low, lo, hi) * (1.0 / C_HEAD_DIM)
            out.append(xb * lax.rsqrt(ms + EPS) * gain[:, c * LANES:(c + 1) * LANES])
        return jnp.concatenate(out, axis=1)

    qn = head_norm(q, qg_ref[...])
    kn = head_norm(k, kg_ref[...])
    q_ref[...] = qn.astype(BF16)
    def by_head(x):
        parts = [x[:, hh * C_HEAD_DIM:(hh + 1) * C_HEAD_DIM] for hh in range(C_HEADS)]
        return pltpu.einshape("hsd->shd", jnp.stack(parts, axis=0))

    kf_ref[...] = by_head(kn)
    vf_ref[...] = by_head(v)
    kb_ref[...] = kn.astype(BF16)
    vb_ref[...] = v.astype(BF16)
    f = lax.dot_general(wf_ref[...], h, (((1,), (1,)), ((), ())),
                        preferred_element_type=F32) + bf_ref[...]
    lf_ref[...] = jnp.minimum(f, 0.0) - jnp.log1p(jnp.exp(-jnp.abs(f)))


def _fox_proj(x2d, g, wqkv, wft, bf, qg, kg):
    n, d = x2d.shape
    tm = min(ROW_TILE, n)
    row = lambda i: (i, 0)
    full = pl.BlockSpec((tm, D_C), row)
    split = pl.BlockSpec((tm, C_HEADS, C_HEAD_DIM), lambda i: (i, 0, 0))
    return pl.pallas_call(
        _fox_proj_kernel,
        grid=(n // tm,),
        in_specs=[pl.BlockSpec((tm, d), row), _resident(g.shape), _resident(wqkv.shape),
                  _resident(wft.shape), _resident(bf.shape), _resident(qg.shape),
                  _resident(kg.shape)],
        out_specs=(full, split, split, full, full, pl.BlockSpec((C_HEADS, tm), lambda i: (0, i))),
        out_shape=(jax.ShapeDtypeStruct((n, D_C), BF16),
                   jax.ShapeDtypeStruct((n, C_HEADS, C_HEAD_DIM), F32),
                   jax.ShapeDtypeStruct((n, C_HEADS, C_HEAD_DIM), F32),
                   jax.ShapeDtypeStruct((n, D_C), BF16),
                   jax.ShapeDtypeStruct((n, D_C), BF16), jax.ShapeDtypeStruct((C_HEADS, n), F32)),
        compiler_params=_params(("arbitrary",)),
        name="fox_proj",
    )(x2d, g, wqkv, wft, bf, qg, kg)


def _cumsum_kernel(x_ref, tri_ref, o_ref, *, sub_last):
    r, n = x_ref.shape
    tri = tri_ref[...]
    off = jnp.zeros((r, 1), F32)
    for i in range(n // LANES):
        xb = x_ref[:, i * LANES:(i + 1) * LANES] * LOG2E
        hi = xb.astype(BF16)
        r1 = xb - hi.astype(F32)
        mid = r1.astype(BF16)
        lo = (r1 - mid.astype(F32)).astype(BF16)
        c = (jnp.dot(hi, tri, preferred_element_type=F32)
             + jnp.dot(mid, tri, preferred_element_type=F32)
             + jnp.dot(lo, tri, preferred_element_type=F32)) + off
        o_ref[:, i * LANES:(i + 1) * LANES] = c
        off = c[:, LANES - 1:LANES]
    if sub_last:
        o_ref[...] = o_ref[...] - off


def _cumsum_lanes(x, sub_last=False):
    tri = (jnp.arange(LANES)[:, None] <= jnp.arange(LANES)[None, :]).astype(BF16)
    return pl.pallas_call(
        functools.partial(_cumsum_kernel, sub_last=sub_last),
        out_shape=jax.ShapeDtypeStruct(x.shape, F32),
        compiler_params=pltpu.CompilerParams(vmem_limit_bytes=VMEM_LIMIT_BYTES),
        name="cumsum_lanes",
    )(x, tri)


def _att_prompt_kernel(q_ref, k_ref, v_ref, dk_ref, tri_ref, o_ref,
                       v0_ref, v1_ref, q0_ref, q1_ref, m_ref, acc_ref):
    hq = pl.program_id(1)
    qi = pl.program_id(2)
    tq = q_ref.shape[0]
    width = q_ref.shape[1]
    pairs = width // LANES
    low = (lax.broadcasted_iota(jnp.int32, (1, width), 1) & (LANES - 1)) < C_HEAD_DIM
    low1 = low[:, 0:LANES]

    @pl.when(qi == 0)
    def _():
        v = v_ref[...]
        one = jnp.ones_like(v)
        v0_ref[...] = jnp.where(low, v, one)
        v1_ref[...] = jnp.where(low, one, v)

    q = q_ref[...]
    zero = jnp.zeros_like(q)
    q0_ref[...] = jnp.where(low, q, zero)
    q1_ref[...] = jnp.where(low, zero, q)
    m_ref[...] = jnp.full(m_ref.shape, NEG_INF, F32)
    acc_ref[...] = jnp.zeros_like(acc_ref)
    qh = (q0_ref, q1_ref)
    vh = (v0_ref, v1_ref)
    nt = (((1,), (1,)), ((), ()))

    def update(c, ks, nk, r0, r1, causal):
        pr, e = divmod(c, 2)
        cols = slice(pr * LANES, (pr + 1) * LANES)
        span = nk * ATT_TK
        s = lax.dot_general(qh[e][r0:r1, cols], k_ref[pl.ds(ks, span), cols], nt,
                            preferred_element_type=F32)
        s = s - dk_ref[pl.ds(2 * pairs * hq + c, 1), pl.ds(ks, span)]
        if causal:
            last = s[:, span - ATT_TK:] + tri_ref[...]
            s = last if nk == 1 else jnp.concatenate([s[:, :span - ATT_TK], last], axis=1)
        m = m_ref[c, r0:r1, :]
        m_new = jnp.maximum(m, jnp.max(s, axis=-1, keepdims=True))
        alpha = jnp.exp2(m - m_new)
        p = jnp.exp2(s - jnp.concatenate([m_new] * (span // LANES), axis=1)).astype(BF16)
        pv = jnp.dot(p, vh[e][pl.ds(ks, span), cols], preferred_element_type=F32)
        acc_ref[c, r0:r1, :] = alpha * acc_ref[c, r0:r1, :] + pv
        m_ref[c, r0:r1, :] = m_new

    per_tile = tq // ATT_TK

    def full_block(j, _):
        ks = pl.multiple_of(j * ATT_TK, ATT_TK)
        for c in range(2 * pairs):
            update(c, ks, 1, 0, tq, False)
        return 0

    lax.fori_loop(0, qi * per_tile, full_block, 0)
    ks0 = pl.multiple_of(qi * tq, ATT_TK)
    for i in range(per_tile):
        for c in range(2 * pairs):
            update(c, ks0, i + 1, i * ATT_TK, (i + 1) * ATT_TK, True)
    for pr in range(pairs):
        acc0, acc1 = acc_ref[2 * pr], acc_ref[2 * pr + 1]
        o0 = acc0 * (1.0 / acc0[:, C_HEAD_DIM:C_HEAD_DIM + 1])
        o1 = acc1 * (1.0 / acc1[:, 0:1])
        o_ref[:, pr * LANES:(pr + 1) * LANES] = jnp.where(low1, o0, o1).astype(BF16)


def _att_prompt(q, k, v, dk):
    b, t, _ = q.shape
    tq = min(ATT_TQ, t)
    nq = t // tq
    width = ATT_HEADS * C_HEAD_DIM
    pos = jnp.arange(ATT_TK)
    tri = jnp.where(pos[None, :] <= pos[:, None], 0.0, NEG_INF).astype(F32)
    return pl.pallas_call(
        _att_prompt_kernel,
        grid=(b, C_HEADS // ATT_HEADS, nq),
        in_specs=[pl.BlockSpec((None, tq, width), lambda bi, hi, qi: (bi, qi, hi)),
                  pl.BlockSpec((None, t, width), lambda bi, hi, qi: (bi, 0, hi)),
                  pl.BlockSpec((None, t, width), lambda bi, hi, qi: (bi, 0, hi)),
                  pl.BlockSpec((None, C_HEADS, t), lambda bi, hi, qi: (bi, 0, 0)),
                  _resident(tri.shape)],
        out_specs=pl.BlockSpec((None, tq, width), lambda bi, hi, qi: (bi, qi, hi)),
        out_shape=jax.ShapeDtypeStruct((b, t, D_C), BF16),
        scratch_shapes=[pltpu.VMEM((t, width), BF16), pltpu.VMEM((t, width), BF16),
                        pltpu.VMEM((tq, width), BF16), pltpu.VMEM((tq, width), BF16),
                        pltpu.VMEM((ATT_HEADS, tq, LANES), F32),
                        pltpu.VMEM((ATT_HEADS, tq, LANES), F32)],
        compiler_params=_params(("arbitrary", "arbitrary", "arbitrary")),
        name="att_prompt",
    )(q, k, v, dk, tri)


def _att_sample_kernel(q_ref, kc_ref, vc_ref, kn_ref, vn_ref, dkc_ref, dkn_ref, o_ref,
                       qrows_ref, m_ref, l_ref, acc_ref, *, t_new):
    ci = pl.program_id(1)
    rows = C_HEADS * t_new
    rhead = lax.broadcasted_iota(jnp.int32, (rows, D_C), 0) // t_new
    lhead = lax.broadcasted_iota(jnp.int32, (rows, D_C), 1) // C_HEAD_DIM
    own = rhead == lhead
    nt = (((1,), (1,)), ((), ()))

    def expand(d):
        return jnp.concatenate(
            [jnp.broadcast_to(d[hh:hh + 1], (t_new, d.shape[1])) for hh in range(C_HEADS)], axis=0)

    def wide(x, n):
        return jnp.concatenate([x] * (n // LANES), axis=1)

    @pl.when(ci == 0)
    def _():
        q = q_ref[...]
        qt = jnp.concatenate([q] * C_HEADS, axis=0)
        qrows = jnp.where(own, qt, jnp.zeros_like(qt))
        qrows_ref[...] = qrows
        s_n = lax.dot_general(qrows, kn_ref[...], nt, preferred_element_type=F32)
        s_n = s_n - expand(dkn_ref[...])[:, 0:t_new]
        tpos = lax.broadcasted_iota(jnp.int32, (rows, t_new), 0) & (t_new - 1)
        spos = lax.broadcasted_iota(jnp.int32, (rows, t_new), 1)
        s_n = jnp.where(spos <= tpos, s_n, NEG_INF)
        m0 = jnp.broadcast_to(jnp.max(s_n, axis=-1, keepdims=True), (rows, LANES))
        p_n = jnp.exp2(s_n - m0[:, 0:t_new])
        m_ref[...] = m0
        l_ref[...] = jnp.broadcast_to(jnp.sum(p_n, axis=-1, keepdims=True), (rows, LANES))
        acc_ref[...] = jnp.dot(p_n.astype(BF16), vn_ref[...], preferred_element_type=F32)

    def rows_by_head(ref):
        x = pltpu.einshape("shd->hsd", ref[...])
        return jnp.concatenate([x[hh].astype(BF16) for hh in range(C_HEADS)], axis=1)

    ts = kc_ref.shape[0]
    s = lax.dot_general(qrows_ref[...], rows_by_head(kc_ref), nt, preferred_element_type=F32)
    s = s - expand(dkc_ref[...])
    m = m_ref[...]
    m_new = jnp.maximum(m, jnp.max(s, axis=-1, keepdims=True))
    alpha = jnp.exp2(m - m_new)
    p = jnp.exp2(s - wide(m_new, ts))
    l_new = alpha * l_ref[...] + jnp.sum(p, axis=-1, keepdims=True)
    acc = wide(alpha, D_C) * acc_ref[...] + jnp.dot(p.astype(BF16), rows_by_head(vc_ref),
                                                    preferred_element_type=F32)
    m_ref[...] = m_new
    l_ref[...] = l_new
    acc_ref[...] = acc

    @pl.when(ci == pl.num_programs(1) - 1)
    def _():
        o_full = jnp.where(own, acc * wide(1.0 / l_new, D_C), 0.0)
        o = o_full[0:t_new]
        for hh in range(1, C_HEADS):
            o = o + o_full[hh * t_new:(hh + 1) * t_new]
        o_ref[...] = o.astype(BF16)


def _att_sample(q, kc, vc, kn, vn, dkc, dkn):
    b, t_new, _ = q.shape
    p_len = kc.shape[1]
    ts = min(SAMPLE_TS, p_len)
    rows = C_HEADS * t_new
    per = lambda bi, ci: (bi, 0, 0)
    cache = pl.BlockSpec((None, ts, C_HEADS, C_HEAD_DIM), lambda bi, ci: (bi, ci, 0, 0))
    return pl.pallas_call(
        functools.partial(_att_sample_kernel, t_new=t_new),
        grid=(b, p_len // ts),
        in_specs=[pl.BlockSpec((None, t_new, D_C), per), cache, cache,
                  pl.BlockSpec((None, t_new, D_C), per), pl.BlockSpec((None, t_new, D_C), per),
                  pl.BlockSpec((None, C_HEADS, ts), lambda bi, ci: (bi, 0, ci)),
                  pl.BlockSpec((None, C_HEADS, LANES), per)],
        out_specs=pl.BlockSpec((None, t_new, D_C), per),
        out_shape=jax.ShapeDtypeStruct((b, t_new, D_C), BF16),
        scratch_shapes=[pltpu.VMEM((rows, D_C), BF16), pltpu.VMEM((rows, LANES), F32),
                        pltpu.VMEM((rows, LANES), F32), pltpu.VMEM((rows, D_C), F32)],
        compiler_params=_params(("arbitrary", "arbitrary")),
        name="att_sample",
    )(q, kc, vc, kn, vn, dkc, dkn)


def _expand_state(state, t):
    b, _, c = state.shape
    p1 = jnp.zeros((b, t, c), F32).at[:, 0].set(state[:, 1])
    p2 = jnp.zeros((b, t, c), F32).at[:, 0].set(state[:, 0]).at[:, 1].set(state[:, 1])
    return p1.reshape(1, b * t, c), p2.reshape(1, b * t, c)


def kernel(x_prompt, x_sample, state_conv_b, state_ffn, cache_k, cache_v, cache_logf,
           norm_mix, norm_ffn, w_in_ab, sgu_norm, w_spatial, b_spatial, conv_b, w_out_ab,
           w_in_c, b_forget, q_norm, k_norm, w_out_c, w_up, conv_ffn, w_down):
    bp, tp, d = x_prompt.shape
    bs, ts, _ = x_sample.shape
    ns = bs * ts
    xs = x_sample.reshape(1, ns, d)

    pos = jnp.arange(GMLP_CHUNK)
    vis = (pos[None, :] // CHUNK) <= (pos[:, None] // CHUNK)
    w_m = jnp.where(vis[None], w_spatial[0], 0.0)
    wm_p = w_m.astype(BF16)
    bb_p = jnp.broadcast_to(b_spatial[0][:, :, None], (A_GROUPS, GMLP_CHUNK, A_GROUP_DIM))
    reps = GMLP_CHUNK // ts
    eye = jnp.eye(reps, dtype=F32)
    wm_s = jnp.einsum('ab,gts->gatbs', eye, w_m[:, :ts, :ts]).reshape(
        A_GROUPS, GMLP_CHUNK, GMLP_CHUNK).astype(BF16)
    bb_s = jnp.broadcast_to(jnp.tile(b_spatial[0][:, :ts], (1, reps))[:, :, None],
                            (A_GROUPS, GMLP_CHUNK, A_GROUP_DIM))
    g_mix0 = norm_mix[0].reshape(1, d)
    win_ab = w_in_ab[0].astype(BF16)
    sg = sgu_norm[0].reshape(1, D_A)
    cwb = conv_b[0]

    y_p, st_b_p = _mixer_ab(x_prompt, g_mix0, win_ab, sg, wm_p, bb_p, cwb)
    pb1, pb2 = _expand_state(state_conv_b[0], ts)
    y_s, c_s, v_s = _mixer_ab(xs, g_mix0, win_ab, sg, wm_s, bb_s, cwb, seg=ts, p1=pb1, p2=pb2)

    def ffn(layer, x_p, yy_p, x_s, yy_s, w_pre):
        g = norm_ffn[layer].reshape(1, d)
        wup = w_up[layer].astype(BF16)
        wdn = w_down[layer].astype(BF16)
        cw = conv_ffn[layer]
        wpre = w_pre.astype(BF16)
        xo_p, st_p = _post(x_p, yy_p, wpre, g, wup, cw, wdn)
        xo_s, st0_s, st1_s = _post(x_s, yy_s, wpre, g, wup, cw, wdn, seg=ts,
                                   s0=state_ffn[layer, :, 0], s1=state_ffn[layer, :, 1])
        st_s = jnp.stack([st0_s, st1_s], axis=1)
        return xo_p, st_p[:, SUBLANES - (CONV_W - 1):], xo_s, st_s

    xp1, ffn_p0, xs1, ffn_s0 = ffn(0, x_prompt, y_p, xs, y_s, w_out_ab[0])

    g_mix1 = norm_mix[1].reshape(1, d)
    wqkv = w_in_c[0][:, :3 * D_C].astype(BF16)
    wft = w_in_c[0][:, 3 * D_C:].T.astype(BF16)
    bf = b_forget[0].reshape(C_HEADS, 1)
    scale = C_HEAD_DIM ** -0.5 * LOG2E
    qg = (jnp.tile(q_norm[0], C_HEADS) * scale).reshape(1, D_C)
    kg = jnp.tile(k_norm[0], C_HEADS).reshape(1, D_C)

    q_p, kf_p, vf_p, kb_p, vb_p, lf_p = _fox_proj(xp1.reshape(bp * tp, d), g_mix1, wqkv, wft, bf,
                                                  qg, kg)
    q_s, kf_s, vf_s, kb_s, vb_s, lf_s = _fox_proj(xs1.reshape(ns, d), g_mix1, wqkv, wft, bf,
                                                  qg, kg)

    lf_p3 = lf_p.reshape(C_HEADS, bp, tp).transpose(1, 0, 2)
    d_p = _cumsum_lanes(lf_p3.reshape(bp * C_HEADS, tp)).reshape(bp, C_HEADS, tp)
    o_p = _att_prompt(q_p.reshape(bp, tp, D_C), kb_p.reshape(bp, tp, D_C),
                      vb_p.reshape(bp, tp, D_C), d_p)

    p_len = cache_k.shape[2]
    lf_s3 = lf_s.reshape(C_HEADS, bs, ts).transpose(1, 0, 2)
    lf_s_pad = jnp.pad(lf_s3, ((0, 0), (0, 0), (0, LANES - ts))).reshape(bs * C_HEADS, LANES)
    dkn = _cumsum_lanes(lf_s_pad).reshape(bs, C_HEADS, LANES)
    lf_c = cache_logf[0].astype(F32).transpose(0, 2, 1).reshape(bs * C_HEADS, p_len)
    dkc = _cumsum_lanes(lf_c, sub_last=True).reshape(bs, C_HEADS, p_len)
    o_s = _att_sample(q_s.reshape(bs, ts, D_C), cache_k[0], cache_v[0],
                      kb_s.reshape(bs, ts, D_C), vb_s.reshape(bs, ts, D_C), dkc, dkn)

    xp2, ffn_p1, xs2, ffn_s1 = ffn(1, xp1, o_p, xs1, o_s.reshape(1, ns, D_C), w_out_c[0])

    heads = (C_HEADS, C_HEAD_DIM)
    return (xp2, xs2.reshape(bs, ts, d),
            st_b_p[None, :, SUBLANES - (CONV_W - 1):],
            c_s.reshape(bs, ts, D_B)[None, :, ts - (CONV_W - 1):],
            v_s.reshape(1, bs, ts, D_A),
            kf_p.reshape(1, bp, tp, *heads), vf_p.reshape(1, bp, tp, *heads),
            lf_p3.transpose(0, 2, 1)[None],
            kf_s.reshape(1, bs, ts, *heads), vf_s.reshape(1, bs, ts, *heads),
            lf_s3.transpose(0, 2, 1)[None],
            jnp.stack([ffn_p0, ffn_p1]), jnp.stack([ffn_s0, ffn_s1]))
```

```python
import functools

import jax
import jax.numpy as jnp
from jax import lax
from jax.experimental import pallas as pl
from jax.experimental.pallas import tpu as pltpu

F32 = jnp.float32
BF16 = jnp.bfloat16

D_MODEL = 1024
CHUNK = 64
GMLP_CHUNK = 128
A_GROUP_DIM = 128
D_A = D_MODEL // 2
A_GROUPS = D_A // A_GROUP_DIM
D_B = D_MODEL // 2
CONV_W = 3
C_HEAD_DIM = 64
C_HEADS = D_MODEL // C_HEAD_DIM
D_C = C_HEADS * C_HEAD_DIM
D_FF = 11 * D_MODEL // 4
EPS = 1e-6
NEG_INF = -1e30

LANES = 128
SUBLANES = 8
MXU_COLS = 256
VMEM_LIMIT_BYTES = 56 * 1024 * 1024
ROW_TILE = 512
FF_CHUNK = MXU_COLS
ATT_TQ = 1024
ATT_HEADS = 4
ATT_TK = MXU_COLS
SAMPLE_TS = 1024
LOG2E = 1.4426950408889634


def _params(sem):
    return pltpu.CompilerParams(dimension_semantics=sem, vmem_limit_bytes=VMEM_LIMIT_BYTES)


def _resident(shape):
    nd = len(shape)
    return pl.BlockSpec(shape, lambda *_: (0,) * nd, pipeline_mode=pl.Buffered(1))


def _rms(x, g):
    ms = jnp.mean(x * x, axis=-1, keepdims=True)
    return x * lax.rsqrt(ms + EPS) * g


def _shift_rows_carry(z, carry):
    r1 = pltpu.roll(z, 1, 0)
    r2 = pltpu.roll(z, 2, 0)
    rid = lax.broadcasted_iota(jnp.int32, (SUBLANES, z.shape[1]), 0)
    h1 = jnp.where(rid < 1, pltpu.roll(carry, 1, 0), r1[0:SUBLANES])
    h2 = jnp.where(rid < 2, pltpu.roll(carry, 2, 0), r2[0:SUBLANES])
    z1 = jnp.concatenate([h1, r1[SUBLANES:]], axis=0)
    z2 = jnp.concatenate([h2, r2[SUBLANES:]], axis=0)
    return z1, z2


def _shift_rows_segments(z, p1, p2, seg):
    rid = lax.broadcasted_iota(jnp.int32, z.shape, 0) & (seg - 1)
    z1 = jnp.where(rid >= 1, pltpu.roll(z, 1, 0), p1)
    z2 = jnp.where(rid >= 2, pltpu.roll(z, 2, 0), p2)
    return z1, z2


def _conv3(z, z1, z2, w):
    return z2 * w[0:1] + z1 * w[1:2] + z * w[2:3]


def _ab_kernel(*refs, tm, seg):
    if seg is None:
        (x_ref, g_ref, win_ref, sg_ref, wm_ref, bb_ref, cw_ref,
         y_ref, st_ref, carry_ref) = refs
    else:
        (x_ref, g_ref, win_ref, sg_ref, wm_ref, bb_ref, cw_ref, p1_ref, p2_ref,
         y_ref, c_ref, v_ref) = refs

    h = _rms(x_ref[...], g_ref[...]).astype(BF16)
    z = jnp.dot(h, win_ref[...], preferred_element_type=F32)

    ya = []
    vn = []
    for g in range(A_GROUPS):
        lo = g * A_GROUP_DIM
        u = jax.nn.gelu(z[:, lo:lo + A_GROUP_DIM])
        vg = jax.nn.gelu(z[:, D_A + lo:D_A + lo + A_GROUP_DIM])
        ms = jnp.mean(vg * vg, axis=-1, keepdims=True)
        vg = vg * lax.rsqrt(ms + EPS) * sg_ref[:, lo:lo + A_GROUP_DIM]
        vn.append(vg)
        vb = vg.astype(BF16)
        wm = wm_ref[g]
        bias = bb_ref[g]
        blocks = []
        for c in range(tm // GMLP_CHUNK):
            blk = vb[c * GMLP_CHUNK:(c + 1) * GMLP_CHUNK]
            blocks.append(jnp.dot(wm, blk, preferred_element_type=F32) + bias)
        ya.append(u * jnp.concatenate(blocks, axis=0))

    off = 2 * D_A
    g_b = z[:, off:off + D_B]
    c = z[:, off + D_B:off + 2 * D_B] * z[:, off + 2 * D_B:off + 3 * D_B]
    if seg is None:
        @pl.when(pl.program_id(1) == 0)
        def _():
            carry_ref[...] = jnp.zeros_like(carry_ref)
        c1, c2 = _shift_rows_carry(c, carry_ref[...])
        carry_ref[...] = c[tm - SUBLANES:tm]
        st_ref[...] = c[tm - SUBLANES:tm]
    else:
        c1, c2 = _shift_rows_segments(c, p1_ref[...], p2_ref[...], seg)
        c_ref[...] = c
        v_ref[...] = jnp.concatenate(vn, axis=1)
    yb = g_b * _conv3(c, c1, c2, cw_ref[...])
    y_ref[...] = jnp.concatenate(ya + [yb], axis=1).astype(BF16)


def _mixer_ab(x, g, win, sg, wm, bb, cw, seg=None, p1=None, p2=None):
    b, t, d = x.shape
    tm = min(ROW_TILE, t)
    nt = t // tm
    row = lambda bi, ti: (bi, ti, 0)
    in_specs = [pl.BlockSpec((None, tm, d), row), _resident(g.shape), _resident(win.shape),
                _resident(sg.shape), _resident(wm.shape), _resident(bb.shape), _resident(cw.shape)]
    args = [x, g, win, sg, wm, bb, cw]
    y_shape = jax.ShapeDtypeStruct((b, t, D_A + D_B), BF16)
    y_spec = pl.BlockSpec((None, tm, D_A + D_B), row)
    if seg is None:
        out_shape = (y_shape, jax.ShapeDtypeStruct((b, SUBLANES, D_B), F32))
        out_specs = (y_spec, pl.BlockSpec((None, SUBLANES, D_B), lambda bi, ti: (bi, 0, 0)))
        scratch = [pltpu.VMEM((SUBLANES, D_B), F32)]
    else:
        in_specs += [pl.BlockSpec((None, tm, D_B), row), pl.BlockSpec((None, tm, D_B), row)]
        args += [p1, p2]
        out_shape = (y_shape, jax.ShapeDtypeStruct((b, t, D_B), F32),
                     jax.ShapeDtypeStruct((b, t, D_A), F32))
        out_specs = (y_spec, pl.BlockSpec((None, tm, D_B), row), pl.BlockSpec((None, tm, D_A), row))
        scratch = []
    return pl.pallas_call(
        functools.partial(_ab_kernel, tm=tm, seg=seg),
        grid=(b, nt), in_specs=in_specs, out_specs=out_specs, out_shape=out_shape,
        scratch_shapes=scratch, compiler_params=_params(("arbitrary", "arbitrary")),
        name="mixer_ab_seg" if seg else "mixer_ab",
    )(*args)


def _post_kernel(*refs, tm, seg):
    if seg is None:
        (x_ref, y_ref, wpre_ref, g_ref, wup_ref, cw_ref, wdn_ref,
         xo_ref, st_ref, a_ref, carry_ref) = refs
    else:
        (x_ref, y_ref, wpre_ref, g_ref, wup_ref, cw_ref, wdn_ref, s0_ref, s1_ref,
         xo_ref, st0_ref, st1_ref, a_ref, p1_ref, p2_ref, zs_ref) = refs
        nseg = tm // seg

    x1 = x_ref[...] + jnp.dot(y_ref[...], wpre_ref[...], preferred_element_type=F32)
    h = _rms(x1, g_ref[...]).astype(BF16)

    if seg is None:
        @pl.when(pl.program_id(1) == 0)
        def _():
            carry_ref[...] = jnp.zeros_like(carry_ref)
    else:
        p1_ref[...] = jnp.zeros_like(p1_ref)
        p2_ref[...] = jnp.zeros_like(p2_ref)

    def conv_cols(lo):
        z = jnp.dot(h, wup_ref[:, lo:lo + FF_CHUNK], preferred_element_type=F32)
        if seg is None:
            z1, z2 = _shift_rows_carry(z, carry_ref[:, lo:lo + FF_CHUNK])
            carry_ref[:, lo:lo + FF_CHUNK] = z[tm - SUBLANES:tm]
            st_ref[:, lo:lo + FF_CHUNK] = z[tm - SUBLANES:tm]
        else:
            for i in range(FF_CHUNK // LANES):
                cols = slice(lo + i * LANES, lo + (i + 1) * LANES)
                p1_ref[i, pl.ds(0, nseg, stride=seg), :] = s1_ref[:, cols]
                p2_ref[i, pl.ds(0, nseg, stride=seg), :] = s0_ref[:, cols]
                p2_ref[i, pl.ds(1, nseg, stride=seg), :] = s1_ref[:, cols]
                zs_ref[i] = z[:, i * LANES:(i + 1) * LANES]
                st0_ref[:, cols] = zs_ref[i, pl.ds(seg - 2, nseg, stride=seg), :]
                st1_ref[:, cols] = zs_ref[i, pl.ds(seg - 1, nseg, stride=seg), :]
            p1 = jnp.concatenate([p1_ref[i] for i in range(FF_CHUNK // LANES)], axis=1)
            p2 = jnp.concatenate([p2_ref[i] for i in range(FF_CHUNK // LANES)], axis=1)
            z1, z2 = _shift_rows_segments(z, p1, p2, seg)
        return _conv3(z, z1, z2, cw_ref[:, lo:lo + FF_CHUNK])

    for j in range(D_FF // FF_CHUNK):
        gate = conv_cols(j * FF_CHUNK)
        up = conv_cols(D_FF + j * FF_CHUNK)
        a_ref[:, j * FF_CHUNK:(j + 1) * FF_CHUNK] = (jax.nn.silu(gate) * up).astype(BF16)

    xo_ref[...] = x1 + jnp.dot(a_ref[...], wdn_ref[...], preferred_element_type=F32)


def _post(x, y, wpre, g, wup, cw, wdn, seg=None, s0=None, s1=None):
    b, t, d = x.shape
    tm = min(ROW_TILE, t)
    nt = t // tm
    assert seg is None or (b == 1 and nt == 1), "packed streams must fit one row tile"
    row = lambda bi, ti: (bi, ti, 0)
    in_specs = [pl.BlockSpec((None, tm, d), row), pl.BlockSpec((None, tm, y.shape[2]), row),
                _resident(wpre.shape), _resident(g.shape), _resident(wup.shape),
                _resident(cw.shape), _resident(wdn.shape)]
    args = [x, y, wpre, g, wup, cw, wdn]
    xo_shape = jax.ShapeDtypeStruct((b, t, d), F32)
    xo_spec = pl.BlockSpec((None, tm, d), row)
    scratch = [pltpu.VMEM((tm, D_FF), BF16)]
    if seg is None:
        out_shape = (xo_shape, jax.ShapeDtypeStruct((b, SUBLANES, 2 * D_FF), F32))
        out_specs = (xo_spec, pl.BlockSpec((None, SUBLANES, 2 * D_FF), lambda bi, ti: (bi, 0, 0)))
        scratch.append(pltpu.VMEM((SUBLANES, 2 * D_FF), F32))
    else:
        in_specs += [_resident(s0.shape), _resident(s1.shape)]
        args += [s0, s1]
        st_shape = jax.ShapeDtypeStruct(s0.shape, F32)
        st_spec = pl.BlockSpec(s0.shape, lambda bi, ti: (0, 0))
        out_shape = (xo_shape, st_shape, st_shape)
        out_specs = (xo_spec, st_spec, st_spec)
        scratch += [pltpu.VMEM((FF_CHUNK // LANES, tm, LANES), F32)] * 3
    return pl.pallas_call(
        functools.partial(_post_kernel, tm=tm, seg=seg),
        grid=(b, nt), in_specs=in_specs, out_specs=out_specs, out_shape=out_shape,
        scratch_shapes=scratch, compiler_params=_params(("arbitrary", "arbitrary")),
        name="post_ffn_seg" if seg else "post_ffn",
    )(*args)


def _fox_proj_kernel(x_ref, g_ref, wq_ref, wk_ref, wv_ref, wf_ref, bf_ref, qg_ref, kg_ref,
                     q_ref, kf_ref, vf_ref, kb_ref, vb_ref, lf_ref, *, transposed):
    h = _rms(x_ref[...], g_ref[...]).astype(BF16)
    nt = (((1,), (1,)), ((), ()))
    low = lax.broadcasted_iota(jnp.int32, (1, LANES), 1) < C_HEAD_DIM

    def head_norm(x, gain):
        out = []
        for c in range(D_C // LANES):
            xb = x[:, c * LANES:(c + 1) * LANES]
            sq = xb * xb
            lo = jnp.sum(jnp.where(low, sq, 0.0), axis=-1, keepdims=True)
            hi = jnp.sum(jnp.where(low, 0.0, sq), axis=-1, keepdims=True)
            ms = jnp.where(low, lo, hi) * (1.0 / C_HEAD_DIM)
            out.append(xb * lax.rsqrt(ms + EPS) * gain[:, c * LANES:(c + 1) * LANES])
        return jnp.concatenate(out, axis=1)

    q = jnp.dot(h, wq_ref[...], preferred_element_type=F32)
    q_ref[...] = head_norm(q, qg_ref[...]).astype(BF16)
    if transposed:
        kt = lax.dot_general(wk_ref[...], h, nt, preferred_element_type=F32)
        tm = kt.shape[1]
        k3 = kt.reshape(C_HEADS, C_HEAD_DIM, tm)
        ms = jnp.mean(k3 * k3, axis=1, keepdims=True)
        kn = (k3 * lax.rsqrt(ms + EPS)).reshape(D_C, tm)
        kn = kn * jnp.concatenate([kg_ref[...]] * (tm // LANES), axis=1)
        v = lax.dot_general(wv_ref[...], h, nt, preferred_element_type=F32)
    else:
        kn = head_norm(jnp.dot(h, wk_ref[...], preferred_element_type=F32), kg_ref[...])
        v = jnp.dot(h, wv_ref[...], preferred_element_type=F32)
    kf_ref[...] = kn
    vf_ref[...] = v
    kb_ref[...] = kn.astype(BF16)
    vb_ref[...] = v.astype(BF16)
    f = lax.dot_general(wf_ref[...], h, nt, preferred_element_type=F32) + bf_ref[...]
    lf_ref[...] = jnp.minimum(f, 0.0) - jnp.log1p(jnp.exp(-jnp.abs(f)))


def _fox_proj(x, g, wq, wk, wv, wft, bf, qg, kg, transposed):
    b, t, d = x.shape
    tm = min(ROW_TILE, t)
    row = pl.BlockSpec((None, tm, D_C), lambda bi, ti: (bi, ti, 0))
    if transposed:
        kv = pl.BlockSpec((None, D_C, tm), lambda bi, ti: (bi, 0, ti))
        kv_shape = (b, D_C, t)
    else:
        kv = row
        kv_shape = (b, t, D_C)
    weights = [g, wq, wk, wv, wft, bf, qg, kg]
    return pl.pallas_call(
        functools.partial(_fox_proj_kernel, transposed=transposed),
        grid=(b, t // tm),
        in_specs=[pl.BlockSpec((None, tm, d), lambda bi, ti: (bi, ti, 0))]
        + [_resident(w.shape) for w in weights],
        out_specs=(row, kv, kv, kv, kv,
                   pl.BlockSpec((None, C_HEADS, tm), lambda bi, ti: (bi, 0, ti))),
        out_shape=(jax.ShapeDtypeStruct((b, t, D_C), BF16),
                   jax.ShapeDtypeStruct(kv_shape, F32), jax.ShapeDtypeStruct(kv_shape, F32),
                   jax.ShapeDtypeStruct(kv_shape, BF16), jax.ShapeDtypeStruct(kv_shape, BF16),
                   jax.ShapeDtypeStruct((b, C_HEADS, t), F32)),
        compiler_params=_params(("arbitrary", "arbitrary")),
        name="fox_proj_t" if transposed else "fox_proj",
    )(x, *weights)


def _cumsum_kernel(x_ref, tri_ref, o_ref, *, sub_last):
    r, n = x_ref.shape
    tri = tri_ref[...]
    off = jnp.zeros((r, 1), F32)
    for i in range(n // LANES):
        xb = x_ref[:, i * LANES:(i + 1) * LANES] * LOG2E
        hi = xb.astype(BF16)
        r1 = xb - hi.astype(F32)
        mid = r1.astype(BF16)
        lo = (r1 - mid.astype(F32)).astype(BF16)
        c = (jnp.dot(hi, tri, preferred_element_type=F32)
             + jnp.dot(mid, tri, preferred_element_type=F32)
             + jnp.dot(lo, tri, preferred_element_type=F32)) + off
        o_ref[:, i * LANES:(i + 1) * LANES] = c
        off = c[:, LANES - 1:LANES]
    if sub_last:
        o_ref[...] = o_ref[...] - off


def _cumsum_lanes(x, sub_last=False):
    tri = (jnp.arange(LANES)[:, None] <= jnp.arange(LANES)[None, :]).astype(BF16)
    return pl.pallas_call(
        functools.partial(_cumsum_kernel, sub_last=sub_last),
        out_shape=jax.ShapeDtypeStruct(x.shape, F32),
        compiler_params=pltpu.CompilerParams(vmem_limit_bytes=VMEM_LIMIT_BYTES),
        name="cumsum_lanes",
    )(x, tri)


def _att_prompt_kernel(q_ref, k_ref, v_ref, dk_ref, tri_ref, o_ref,
                       v0_ref, v1_ref, q0_ref, q1_ref, m_ref, acc_ref):
    hq = pl.program_id(1)
    qi = pl.program_id(2)
    tq = q_ref.shape[0]
    width = q_ref.shape[1]
    pairs = width // LANES
    low = (lax.broadcasted_iota(jnp.int32, (1, width), 1) & (LANES - 1)) < C_HEAD_DIM
    low1 = low[:, 0:LANES]
    low_rows = (lax.broadcasted_iota(jnp.int32, (width, 1), 0) & (LANES - 1)) < C_HEAD_DIM

    @pl.when(qi == 0)
    def _():
        v = v_ref[...]
        one = jnp.ones_like(v)
        v0_ref[...] = jnp.where(low_rows, v, one)
        v1_ref[...] = jnp.where(low_rows, one, v)

    q = q_ref[...]
    zero = jnp.zeros_like(q)
    q0_ref[...] = jnp.where(low, q, zero)
    q1_ref[...] = jnp.where(low, zero, q)
    m_ref[...] = jnp.full(m_ref.shape, NEG_INF, F32)
    acc_ref[...] = jnp.zeros_like(acc_ref)
    qh = (q0_ref, q1_ref)
    vh = (v0_ref, v1_ref)
    nt = (((1,), (1,)), ((), ()))

    def update(c, ks, nk, r0, r1, causal):
        pr, e = divmod(c, 2)
        cols = slice(pr * LANES, (pr + 1) * LANES)
        span = nk * ATT_TK
        s = jnp.dot(qh[e][r0:r1, cols], k_ref[cols, pl.ds(ks, span)],
                    preferred_element_type=F32)
        s = s - dk_ref[pl.ds(2 * pairs * hq + c, 1), pl.ds(ks, span)]
        if causal:
            last = s[:, span - ATT_TK:] + tri_ref[...]
            s = last if nk == 1 else jnp.concatenate([s[:, :span - ATT_TK], last], axis=1)
        m = m_ref[c, r0:r1, :]
        m_new = jnp.maximum(m, jnp.max(s, axis=-1, keepdims=True))
        alpha = jnp.exp2(m - m_new)
        p = jnp.exp2(s - jnp.concatenate([m_new] * (span // LANES), axis=1)).astype(BF16)
        pv = lax.dot_general(p, vh[e][cols, pl.ds(ks, span)], nt, preferred_element_type=F32)
        acc_ref[c, r0:r1, :] = alpha * acc_ref[c, r0:r1, :] + pv
        m_ref[c, r0:r1, :] = m_new

    per_tile = tq // ATT_TK

    def full_block(j, _):
        ks = pl.multiple_of(j * ATT_TK, ATT_TK)
        for c in range(2 * pairs):
            update(c, ks, 1, 0, tq, False)
        return 0

    lax.fori_loop(0, qi * per_tile, full_block, 0)
    ks0 = pl.multiple_of(qi * tq, ATT_TK)
    for i in range(per_tile):
        for c in range(2 * pairs):
            update(c, ks0, i + 1, i * ATT_TK, (i + 1) * ATT_TK, True)
    for pr in range(pairs):
        acc0, acc1 = acc_ref[2 * pr], acc_ref[2 * pr + 1]
        o0 = acc0 * (1.0 / acc0[:, C_HEAD_DIM:C_HEAD_DIM + 1])
        o1 = acc1 * (1.0 / acc1[:, 0:1])
        o_ref[:, pr * LANES:(pr + 1) * LANES] = jnp.where(low1, o0, o1).astype(BF16)


def _att_prompt(q, k, v, dk):
    b, t, _ = q.shape
    tq = min(ATT_TQ, t)
    nq = t // tq
    width = ATT_HEADS * C_HEAD_DIM
    pos = jnp.arange(ATT_TK)
    tri = jnp.where(pos[None, :] <= pos[:, None], 0.0, NEG_INF).astype(F32)
    return pl.pallas_call(
        _att_prompt_kernel,
        grid=(b, C_HEADS // ATT_HEADS, nq),
        in_specs=[pl.BlockSpec((None, tq, width), lambda bi, hi, qi: (bi, qi, hi)),
                  pl.BlockSpec((None, width, t), lambda bi, hi, qi: (bi, hi, 0)),
                  pl.BlockSpec((None, width, t), lambda bi, hi, qi: (bi, hi, 0)),
                  pl.BlockSpec((None, C_HEADS, t), lambda bi, hi, qi: (bi, 0, 0)),
                  _resident(tri.shape)],
        out_specs=pl.BlockSpec((None, tq, width), lambda bi, hi, qi: (bi, qi, hi)),
        out_shape=jax.ShapeDtypeStruct((b, t, D_C), BF16),
        scratch_shapes=[pltpu.VMEM((width, t), BF16), pltpu.VMEM((width, t), BF16),
                        pltpu.VMEM((tq, width), BF16), pltpu.VMEM((tq, width), BF16),
                        pltpu.VMEM((ATT_HEADS, tq, LANES), F32),
                        pltpu.VMEM((ATT_HEADS, tq, LANES), F32)],
        compiler_params=_params(("arbitrary", "arbitrary", "arbitrary")),
        name="att_prompt",
    )(q, k, v, dk, tri)


def _att_sample_kernel(q_ref, kc_ref, vc_ref, kn_ref, vn_ref, dkc_ref, dkn_ref, o_ref,
                       qrows_ref, m_ref, l_ref, acc_ref, *, t_new):
    ci = pl.program_id(1)
    rows = C_HEADS * t_new
    rhead = lax.broadcasted_iota(jnp.int32, (rows, D_C), 0) // t_new
    lhead = lax.broadcasted_iota(jnp.int32, (rows, D_C), 1) // C_HEAD_DIM
    own = rhead == lhead
    nt = (((1,), (1,)), ((), ()))

    def expand(d):
        return jnp.concatenate(
            [jnp.broadcast_to(d[hh:hh + 1], (t_new, d.shape[1])) for hh in range(C_HEADS)], axis=0)

    def wide(x, n):
        return jnp.concatenate([x] * (n // LANES), axis=1)

    @pl.when(ci == 0)
    def _():
        q = q_ref[...]
        qt = jnp.concatenate([q] * C_HEADS, axis=0)
        qrows = jnp.where(own, qt, jnp.zeros_like(qt))
        qrows_ref[...] = qrows
        s_n = lax.dot_general(qrows, kn_ref[...], nt, preferred_element_type=F32)
        s_n = s_n - expand(dkn_ref[...])[:, 0:t_new]
        tpos = lax.broadcasted_iota(jnp.int32, (rows, t_new), 0) & (t_new - 1)
        spos = lax.broadcasted_iota(jnp.int32, (rows, t_new), 1)
        s_n = jnp.where(spos <= tpos, s_n, NEG_INF)
        m0 = jnp.broadcast_to(jnp.max(s_n, axis=-1, keepdims=True), (rows, LANES))
        p_n = jnp.exp2(s_n - m0[:, 0:t_new])
        m_ref[...] = m0
        l_ref[...] = jnp.broadcast_to(jnp.sum(p_n, axis=-1, keepdims=True), (rows, LANES))
        acc_ref[...] = jnp.dot(p_n.astype(BF16), vn_ref[...], preferred_element_type=F32)

    ts = kc_ref.shape[1]
    s = jnp.dot(qrows_ref[...], kc_ref[...].astype(BF16), preferred_element_type=F32)
    s = s - expand(dkc_ref[...])
    m = m_ref[...]
    m_new = jnp.maximum(m, jnp.max(s, axis=-1, keepdims=True))
    alpha = jnp.exp2(m - m_new)
    p = jnp.exp2(s - wide(m_new, ts))
    l_new = alpha * l_ref[...] + jnp.sum(p, axis=-1, keepdims=True)
    acc = wide(alpha, D_C) * acc_ref[...] + lax.dot_general(
        p.astype(BF16), vc_ref[...].astype(BF16), nt, preferred_element_type=F32)
    m_ref[...] = m_new
    l_ref[...] = l_new
    acc_ref[...] = acc

    @pl.when(ci == pl.num_programs(1) - 1)
    def _():
        o_full = jnp.where(own, acc * wide(1.0 / l_new, D_C), 0.0)
        o = o_full[0:t_new]
        for hh in range(1, C_HEADS):
            o = o + o_full[hh * t_new:(hh + 1) * t_new]
        o_ref[...] = o.astype(BF16)


def _att_sample(q, kc, vc, kn, vn, dkc, dkn):
    b, t_new, _ = q.shape
    p_len = kc.shape[2]
    ts = min(SAMPLE_TS, p_len)
    rows = C_HEADS * t_new
    per = lambda bi, ci: (bi, 0, 0)
    cache = pl.BlockSpec((None, D_C, ts), lambda bi, ci: (bi, 0, ci))
    return pl.pallas_call(
        functools.partial(_att_sample_kernel, t_new=t_new),
        grid=(b, p_len // ts),
        in_specs=[pl.BlockSpec((None, t_new, D_C), per), cache, cache,
                  pl.BlockSpec((None, t_new, D_C), per), pl.BlockSpec((None, t_new, D_C), per),
                  pl.BlockSpec((None, C_HEADS, ts), lambda bi, ci: (bi, 0, ci)),
                  pl.BlockSpec((None, C_HEADS, LANES), per)],
        out_specs=pl.BlockSpec((None, t_new, D_C), per),
        out_shape=jax.ShapeDtypeStruct((b, t_new, D_C), BF16),
        scratch_shapes=[pltpu.VMEM((rows, D_C), BF16), pltpu.VMEM((rows, LANES), F32),
                        pltpu.VMEM((rows, LANES), F32), pltpu.VMEM((rows, D_C), F32)],
        compiler_params=_params(("arbitrary", "arbitrary")),
        name="att_sample",
    )(q, kc, vc, kn, vn, dkc, dkn)


def _expand_state(state, t):
    b, _, c = state.shape
    p1 = jnp.zeros((b, t, c), F32).at[:, 0].set(state[:, 1])
    p2 = jnp.zeros((b, t, c), F32).at[:, 0].set(state[:, 0]).at[:, 1].set(state[:, 1])
    return p1.reshape(1, b * t, c), p2.reshape(1, b * t, c)


def kernel(x_prompt, x_sample, state_conv_b, state_ffn, cache_k, cache_v, cache_logf,
           norm_mix, norm_ffn, w_in_ab, sgu_norm, w_spatial, b_spatial, conv_b, w_out_ab,
           w_in_c, b_forget, q_norm, k_norm, w_out_c, w_up, conv_ffn, w_down):
    bp, tp, d = x_prompt.shape
    bs, ts, _ = x_sample.shape
    ns = bs * ts
    xs = x_sample.reshape(1, ns, d)

    pos = jnp.arange(GMLP_CHUNK)
    vis = (pos[None, :] // CHUNK) <= (pos[:, None] // CHUNK)
    w_m = jnp.where(vis[None], w_spatial[0], 0.0)
    wm_p = w_m.astype(BF16)
    bb_p = jnp.broadcast_to(b_spatial[0][:, :, None], (A_GROUPS, GMLP_CHUNK, A_GROUP_DIM))
    reps = GMLP_CHUNK // ts
    eye = jnp.eye(reps, dtype=F32)
    wm_s = jnp.einsum('ab,gts->gatbs', eye, w_m[:, :ts, :ts]).reshape(
        A_GROUPS, GMLP_CHUNK, GMLP_CHUNK).astype(BF16)
    bb_s = jnp.broadcast_to(jnp.tile(b_spatial[0][:, :ts], (1, reps))[:, :, None],
                            (A_GROUPS, GMLP_CHUNK, A_GROUP_DIM))
    g_mix0 = norm_mix[0].reshape(1, d)
    win_ab = w_in_ab[0].astype(BF16)
    sg = sgu_norm[0].reshape(1, D_A)
    cwb = conv_b[0]

    y_p, st_b_p = _mixer_ab(x_prompt, g_mix0, win_ab, sg, wm_p, bb_p, cwb)
    pb1, pb2 = _expand_state(state_conv_b[0], ts)
    y_s, c_s, v_s = _mixer_ab(xs, g_mix0, win_ab, sg, wm_s, bb_s, cwb, seg=ts, p1=pb1, p2=pb2)

    def ffn(layer, x_p, yy_p, x_s, yy_s, w_pre):
        g = norm_ffn[layer].reshape(1, d)
        wup = w_up[layer].astype(BF16)
        wdn = w_down[layer].astype(BF16)
        cw = conv_ffn[layer]
        wpre = w_pre.astype(BF16)
        xo_p, st_p = _post(x_p, yy_p, wpre, g, wup, cw, wdn)
        xo_s, st0_s, st1_s = _post(x_s, yy_s, wpre, g, wup, cw, wdn, seg=ts,
                                   s0=state_ffn[layer, :, 0], s1=state_ffn[layer, :, 1])
        st_s = jnp.stack([st0_s, st1_s], axis=1)
        return xo_p, st_p[:, SUBLANES - (CONV_W - 1):], xo_s, st_s

    xp1, ffn_p0, xs1, ffn_s0 = ffn(0, x_prompt, y_p, xs, y_s, w_out_ab[0])

    g_mix1 = norm_mix[1].reshape(1, d)
    w_c = w_in_c[0]
    wq = w_c[:, 0:D_C].astype(BF16)
    wk = w_c[:, D_C:2 * D_C].astype(BF16)
    wv = w_c[:, 2 * D_C:3 * D_C].astype(BF16)
    wft = w_c[:, 3 * D_C:].T.astype(BF16)
    bf = b_forget[0].reshape(C_HEADS, 1)
    scale = C_HEAD_DIM ** -0.5 * LOG2E
    qg = (jnp.tile(q_norm[0], C_HEADS) * scale).reshape(1, D_C)
    kg = jnp.tile(k_norm[0], C_HEADS)
    kg_rows = jnp.broadcast_to(kg[:, None], (D_C, LANES))

    q_p, kf_p, vf_p, kb_p, vb_p, lf_p = _fox_proj(xp1, g_mix1, wq, wk.T, wv.T, wft, bf, qg,
                                                  kg_rows, transposed=True)
    q_s, kf_s, vf_s, kb_s, vb_s, lf_s = _fox_proj(xs1, g_mix1, wq, wk, wv, wft, bf, qg,
                                                  kg.reshape(1, D_C), transposed=False)

    d_p = _cumsum_lanes(lf_p.reshape(bp * C_HEADS, tp)).reshape(bp, C_HEADS, tp)
    o_p = _att_prompt(q_p, kb_p, vb_p, d_p)

    p_len = cache_k.shape[2]
    lf_s3 = lf_s.reshape(C_HEADS, bs, ts).transpose(1, 0, 2)
    lf_s_pad = jnp.pad(lf_s3, ((0, 0), (0, 0), (0, LANES - ts))).reshape(bs * C_HEADS, LANES)
    dkn = _cumsum_lanes(lf_s_pad).reshape(bs, C_HEADS, LANES)
    lf_c = cache_logf[0].astype(F32).transpose(0, 2, 1).reshape(bs * C_HEADS, p_len)
    dkc = _cumsum_lanes(lf_c, sub_last=True).reshape(bs, C_HEADS, p_len)
    kc = cache_k[0].transpose(0, 2, 3, 1).reshape(bs, D_C, p_len)
    vc = cache_v[0].transpose(0, 2, 3, 1).reshape(bs, D_C, p_len)
    o_s = _att_sample(q_s.reshape(bs, ts, D_C), kc, vc,
                      kb_s.reshape(bs, ts, D_C), vb_s.reshape(bs, ts, D_C), dkc, dkn)

    xp2, ffn_p1, xs2, ffn_s1 = ffn(1, xp1, o_p, xs1, o_s.reshape(1, ns, D_C), w_out_c[0])

    heads = (C_HEADS, C_HEAD_DIM)

    def by_position(xt):
        return xt.reshape(bp, C_HEADS, C_HEAD_DIM, tp).transpose(0, 3, 1, 2)[None]

    return (xp2, xs2.reshape(bs, ts, d),
            st_b_p[None, :, SUBLANES - (CONV_W - 1):],
            c_s.reshape(bs, ts, D_B)[None, :, ts - (CONV_W - 1):],
            v_s.reshape(1, bs, ts, D_A),
            by_position(kf_p), by_position(vf_p),
            lf_p.transpose(0, 2, 1)[None],
            kf_s.reshape(1, bs, ts, *heads), vf_s.reshape(1, bs, ts, *heads),
            lf_s3.transpose(0, 2, 1)[None],
            jnp.stack([ffn_p0, ffn_p1]), jnp.stack([ffn_s0, ffn_s1]))
```

```python
import functools

import jax
import jax.numpy as jnp
from jax import lax
from jax.experimental import pallas as pl
from jax.experimental.pallas import tpu as pltpu

F32 = jnp.float32
BF16 = jnp.bfloat16

D_MODEL = 1024
CHUNK = 64
GMLP_CHUNK = 128
A_GROUP_DIM = 128
D_A = D_MODEL // 2
A_GROUPS = D_A // A_GROUP_DIM
D_B = D_MODEL // 2
CONV_W = 3
C_HEAD_DIM = 64
C_HEADS = D_MODEL // C_HEAD_DIM
D_C = C_HEADS * C_HEAD_DIM
D_FF = 11 * D_MODEL // 4
EPS = 1e-6
NEG_INF = -1e30

LANES = 128
SUBLANES = 8
MXU_COLS = 256
VMEM_LIMIT_BYTES = 56 * 1024 * 1024
ROW_TILE = 512
FF_CHUNK = MXU_COLS
ATT_TQ = 1024
ATT_HEADS = 4
ATT_TK = MXU_COLS
ATT_BAND = 512
SAMPLE_TS = 1024
LOG2E = 1.4426950408889634


def _params(sem):
    return pltpu.CompilerParams(dimension_semantics=sem, vmem_limit_bytes=VMEM_LIMIT_BYTES)


def _resident(shape):
    nd = len(shape)
    return pl.BlockSpec(shape, lambda *_: (0,) * nd, pipeline_mode=pl.Buffered(1))


def _rms(x, g):
    ms = jnp.mean(x * x, axis=-1, keepdims=True)
    return x * lax.rsqrt(ms + EPS) * g


def _shift_rows_carry(z, carry):
    r1 = pltpu.roll(z, 1, 0)
    r2 = pltpu.roll(z, 2, 0)
    rid = lax.broadcasted_iota(jnp.int32, (SUBLANES, z.shape[1]), 0)
    h1 = jnp.where(rid < 1, pltpu.roll(carry, 1, 0), r1[0:SUBLANES])
    h2 = jnp.where(rid < 2, pltpu.roll(carry, 2, 0), r2[0:SUBLANES])
    z1 = jnp.concatenate([h1, r1[SUBLANES:]], axis=0)
    z2 = jnp.concatenate([h2, r2[SUBLANES:]], axis=0)
    return z1, z2


def _shift_rows_segments(z, p1, p2, seg):
    rid = lax.broadcasted_iota(jnp.int32, z.shape, 0) & (seg - 1)
    z1 = jnp.where(rid >= 1, pltpu.roll(z, 1, 0), p1)
    z2 = jnp.where(rid >= 2, pltpu.roll(z, 2, 0), p2)
    return z1, z2


def _conv3(z, z1, z2, w):
    return z2 * w[0:1] + z1 * w[1:2] + z * w[2:3]


def _ab_kernel(*refs, tm, seg):
    if seg is None:
        (x_ref, g_ref, win_ref, sg_ref, wm_ref, bb_ref, cw_ref,
         y_ref, st_ref, carry_ref) = refs
    else:
        (x_ref, g_ref, win_ref, sg_ref, wm_ref, bb_ref, cw_ref, p1_ref, p2_ref,
         y_ref, c_ref, v_ref) = refs

    h = _rms(x_ref[...], g_ref[...]).astype(BF16)
    z = jnp.dot(h, win_ref[...], preferred_element_type=F32)

    ya = []
    vn = []
    for g in range(A_GROUPS):
        lo = g * A_GROUP_DIM
        u = jax.nn.gelu(z[:, lo:lo + A_GROUP_DIM])
        vg = jax.nn.gelu(z[:, D_A + lo:D_A + lo + A_GROUP_DIM])
        ms = jnp.mean(vg * vg, axis=-1, keepdims=True)
        vg = vg * lax.rsqrt(ms + EPS) * sg_ref[:, lo:lo + A_GROUP_DIM]
        vn.append(vg)
        vb = vg.astype(BF16)
        wm = wm_ref[g]
        bias = bb_ref[g]
        blocks = []
        for c in range(tm // GMLP_CHUNK):
            blk = vb[c * GMLP_CHUNK:(c + 1) * GMLP_CHUNK]
            blocks.append(jnp.dot(wm, blk, preferred_element_type=F32) + bias)
        ya.append(u * jnp.concatenate(blocks, axis=0))

    off = 2 * D_A
    g_b = z[:, off:off + D_B]
    c = z[:, off + D_B:off + 2 * D_B] * z[:, off + 2 * D_B:off + 3 * D_B]
    if seg is None:
        @pl.when(pl.program_id(1) == 0)
        def _():
            carry_ref[...] = jnp.zeros_like(carry_ref)
        c1, c2 = _shift_rows_carry(c, carry_ref[...])
        carry_ref[...] = c[tm - SUBLANES:tm]
        st_ref[...] = c[tm - SUBLANES:tm]
    else:
        c1, c2 = _shift_rows_segments(c, p1_ref[...], p2_ref[...], seg)
        c_ref[...] = c
        v_ref[...] = jnp.concatenate(vn, axis=1)
    yb = g_b * _conv3(c, c1, c2, cw_ref[...])
    y_ref[...] = jnp.concatenate(ya + [yb], axis=1).astype(BF16)


def _mixer_ab(x, g, win, sg, wm, bb, cw, seg=None, p1=None, p2=None):
    b, t, d = x.shape
    tm = min(ROW_TILE, t)
    nt = t // tm
    row = lambda bi, ti: (bi, ti, 0)
    in_specs = [pl.BlockSpec((None, tm, d), row), _resident(g.shape), _resident(win.shape),
                _resident(sg.shape), _resident(wm.shape), _resident(bb.shape), _resident(cw.shape)]
    args = [x, g, win, sg, wm, bb, cw]
    y_shape = jax.ShapeDtypeStruct((b, t, D_A + D_B), BF16)
    y_spec = pl.BlockSpec((None, tm, D_A + D_B), row)
    if seg is None:
        out_shape = (y_shape, jax.ShapeDtypeStruct((b, SUBLANES, D_B), F32))
        out_specs = (y_spec, pl.BlockSpec((None, SUBLANES, D_B), lambda bi, ti: (bi, 0, 0)))
        scratch = [pltpu.VMEM((SUBLANES, D_B), F32)]
    else:
        in_specs += [pl.BlockSpec((None, tm, D_B), row), pl.BlockSpec((None, tm, D_B), row)]
        args += [p1, p2]
        out_shape = (y_shape, jax.ShapeDtypeStruct((b, t, D_B), F32),
                     jax.ShapeDtypeStruct((b, t, D_A), F32))
        out_specs = (y_spec, pl.BlockSpec((None, tm, D_B), row), pl.BlockSpec((None, tm, D_A), row))
        scratch = []
    return pl.pallas_call(
        functools.partial(_ab_kernel, tm=tm, seg=seg),
        grid=(b, nt), in_specs=in_specs, out_specs=out_specs, out_shape=out_shape,
        scratch_shapes=scratch, compiler_params=_params(("arbitrary", "arbitrary")),
        name="mixer_ab_seg" if seg else "mixer_ab",
    )(*args)


def _post_kernel(*refs, tm, seg):
    if seg is None:
        (x_ref, y_ref, wpre_ref, g_ref, wup_ref, cw_ref, wdn_ref,
         xo_ref, st_ref, a_ref, carry_ref) = refs
    else:
        (x_ref, y_ref, wpre_ref, g_ref, wup_ref, cw_ref, wdn_ref, s0_ref, s1_ref,
         xo_ref, st0_ref, st1_ref, a_ref, p1_ref, p2_ref, zs_ref) = refs
        nseg = tm // seg

    x1 = x_ref[...] + jnp.dot(y_ref[...], wpre_ref[...], preferred_element_type=F32)
    h = _rms(x1, g_ref[...]).astype(BF16)

    if seg is None:
        @pl.when(pl.program_id(1) == 0)
        def _():
            carry_ref[...] = jnp.zeros_like(carry_ref)
    else:
        p1_ref[...] = jnp.zeros_like(p1_ref)
        p2_ref[...] = jnp.zeros_like(p2_ref)

    def conv_cols(lo):
        z = jnp.dot(h, wup_ref[:, lo:lo + FF_CHUNK], preferred_element_type=F32)
        if seg is None:
            z1, z2 = _shift_rows_carry(z, carry_ref[:, lo:lo + FF_CHUNK])
            carry_ref[:, lo:lo + FF_CHUNK] = z[tm - SUBLANES:tm]
            st_ref[:, lo:lo + FF_CHUNK] = z[tm - SUBLANES:tm]
        else:
            for i in range(FF_CHUNK // LANES):
                cols = slice(lo + i * LANES, lo + (i + 1) * LANES)
                p1_ref[i, pl.ds(0, nseg, stride=seg), :] = s1_ref[:, cols]
                p2_ref[i, pl.ds(0, nseg, stride=seg), :] = s0_ref[:, cols]
                p2_ref[i, pl.ds(1, nseg, stride=seg), :] = s1_ref[:, cols]
                zs_ref[i] = z[:, i * LANES:(i + 1) * LANES]
                st0_ref[:, cols] = zs_ref[i, pl.ds(seg - 2, nseg, stride=seg), :]
                st1_ref[:, cols] = zs_ref[i, pl.ds(seg - 1, nseg, stride=seg), :]
            p1 = jnp.concatenate([p1_ref[i] for i in range(FF_CHUNK // LANES)], axis=1)
            p2 = jnp.concatenate([p2_ref[i] for i in range(FF_CHUNK // LANES)], axis=1)
            z1, z2 = _shift_rows_segments(z, p1, p2, seg)
        return _conv3(z, z1, z2, cw_ref[:, lo:lo + FF_CHUNK])

    for j in range(D_FF // FF_CHUNK):
        gate = conv_cols(j * FF_CHUNK)
        up = conv_cols(D_FF + j * FF_CHUNK)
        a_ref[:, j * FF_CHUNK:(j + 1) * FF_CHUNK] = (jax.nn.silu(gate) * up).astype(BF16)

    xo_ref[...] = x1 + jnp.dot(a_ref[...], wdn_ref[...], preferred_element_type=F32)


def _post(x, y, wpre, g, wup, cw, wdn, seg=None, s0=None, s1=None):
    b, t, d = x.shape
    tm = min(ROW_TILE, t)
    nt = t // tm
    assert seg is None or (b == 1 and nt == 1), "packed streams must fit one row tile"
    row = lambda bi, ti: (bi, ti, 0)
    in_specs = [pl.BlockSpec((None, tm, d), row), pl.BlockSpec((None, tm, y.shape[2]), row),
                _resident(wpre.shape), _resident(g.shape), _resident(wup.shape),
                _resident(cw.shape), _resident(wdn.shape)]
    args = [x, y, wpre, g, wup, cw, wdn]
    xo_shape = jax.ShapeDtypeStruct((b, t, d), F32)
    xo_spec = pl.BlockSpec((None, tm, d), row)
    scratch = [pltpu.VMEM((tm, D_FF), BF16)]
    if seg is None:
        out_shape = (xo_shape, jax.ShapeDtypeStruct((b, SUBLANES, 2 * D_FF), F32))
        out_specs = (xo_spec, pl.BlockSpec((None, SUBLANES, 2 * D_FF), lambda bi, ti: (bi, 0, 0)))
        scratch.append(pltpu.VMEM((SUBLANES, 2 * D_FF), F32))
    else:
        in_specs += [_resident(s0.shape), _resident(s1.shape)]
        args += [s0, s1]
        st_shape = jax.ShapeDtypeStruct(s0.shape, F32)
        st_spec = pl.BlockSpec(s0.shape, lambda bi, ti: (0, 0))
        out_shape = (xo_shape, st_shape, st_shape)
        out_specs = (xo_spec, st_spec, st_spec)
        scratch += [pltpu.VMEM((FF_CHUNK // LANES, tm, LANES), F32)] * 3
    return pl.pallas_call(
        functools.partial(_post_kernel, tm=tm, seg=seg),
        grid=(b, nt), in_specs=in_specs, out_specs=out_specs, out_shape=out_shape,
        scratch_shapes=scratch, compiler_params=_params(("arbitrary", "arbitrary")),
        name="post_ffn_seg" if seg else "post_ffn",
    )(*args)


def _fox_proj_kernel(x_ref, g_ref, wq_ref, wk_ref, wv_ref, wf_ref, bf_ref, qg_ref, kg_ref,
                     q_ref, kf_ref, vf_ref, kb_ref, vb_ref, lf_ref, *, transposed):
    h = _rms(x_ref[...], g_ref[...]).astype(BF16)
    nt = (((1,), (1,)), ((), ()))
    low = lax.broadcasted_iota(jnp.int32, (1, LANES), 1) < C_HEAD_DIM

    def head_norm(x, gain):
        out = []
        for c in range(D_C // LANES):
            xb = x[:, c * LANES:(c + 1) * LANES]
            sq = xb * xb
            lo = jnp.sum(jnp.where(low, sq, 0.0), axis=-1, keepdims=True)
            hi = jnp.sum(jnp.where(low, 0.0, sq), axis=-1, keepdims=True)
            ms = jnp.where(low, lo, hi) * (1.0 / C_HEAD_DIM)
            out.append(xb * lax.rsqrt(ms + EPS) * gain[:, c * LANES:(c + 1) * LANES])
        return jnp.concatenate(out, axis=1)

    q = jnp.dot(h, wq_ref[...], preferred_element_type=F32)
    q_ref[...] = head_norm(q, qg_ref[...]).astype(BF16)
    if transposed:
        kt = lax.dot_general(wk_ref[...], h, nt, preferred_element_type=F32)
        tm = kt.shape[1]
        k3 = kt.reshape(C_HEADS, C_HEAD_DIM, tm)
        ms = jnp.mean(k3 * k3, axis=1, keepdims=True)
        kn = (k3 * lax.rsqrt(ms + EPS)).reshape(D_C, tm)
        kn = kn * jnp.concatenate([kg_ref[...]] * (tm // LANES), axis=1)
        v = lax.dot_general(wv_ref[...], h, nt, preferred_element_type=F32)
    else:
        kn = head_norm(jnp.dot(h, wk_ref[...], preferred_element_type=F32), kg_ref[...])
        v = jnp.dot(h, wv_ref[...], preferred_element_type=F32)
    kf_ref[...] = kn
    vf_ref[...] = v
    kb_ref[...] = kn.astype(BF16)
    vb_ref[...] = v.astype(BF16)
    f = lax.dot_general(wf_ref[...], h, nt, preferred_element_type=F32) + bf_ref[...]
    lf_ref[...] = jnp.minimum(f, 0.0) - jnp.log1p(jnp.exp(-jnp.abs(f)))


def _fox_proj(x, g, wq, wk, wv, wft, bf, qg, kg, transposed):
    b, t, d = x.shape
    tm = min(ROW_TILE, t)
    row = pl.BlockSpec((None, tm, D_C), lambda bi, ti: (bi, ti, 0))
    if transposed:
        kv = pl.BlockSpec((None, D_C, tm), lambda bi, ti: (bi, 0, ti))
        kv_shape = (b, D_C, t)
    else:
        kv = row
        kv_shape = (b, t, D_C)
    weights = [g, wq, wk, wv, wft, bf, qg, kg]
    return pl.pallas_call(
        functools.partial(_fox_proj_kernel, transposed=transposed),
        grid=(b, t // tm),
        in_specs=[pl.BlockSpec((None, tm, d), lambda bi, ti: (bi, ti, 0))]
        + [_resident(w.shape) for w in weights],
        out_specs=(row, kv, kv, kv, kv,
                   pl.BlockSpec((None, C_HEADS, tm), lambda bi, ti: (bi, 0, ti))),
        out_shape=(jax.ShapeDtypeStruct((b, t, D_C), BF16),
                   jax.ShapeDtypeStruct(kv_shape, F32), jax.ShapeDtypeStruct(kv_shape, F32),
                   jax.ShapeDtypeStruct(kv_shape, BF16), jax.ShapeDtypeStruct(kv_shape, BF16),
                   jax.ShapeDtypeStruct((b, C_HEADS, t), F32)),
        compiler_params=_params(("arbitrary", "arbitrary")),
        name="fox_proj_t" if transposed else "fox_proj",
    )(x, *weights)


def _cumsum_kernel(x_ref, tri_ref, o_ref, *, sub_last):
    r, n = x_ref.shape
    tri = tri_ref[...]
    off = jnp.zeros((r, 1), F32)
    for i in range(n // LANES):
        xb = x_ref[:, i * LANES:(i + 1) * LANES] * LOG2E
        hi = xb.astype(BF16)
        r1 = xb - hi.astype(F32)
        mid = r1.astype(BF16)
        lo = (r1 - mid.astype(F32)).astype(BF16)
        c = (jnp.dot(hi, tri, preferred_element_type=F32)
             + jnp.dot(mid, tri, preferred_element_type=F32)
             + jnp.dot(lo, tri, preferred_element_type=F32)) + off
        o_ref[:, i * LANES:(i + 1) * LANES] = c
        off = c[:, LANES - 1:LANES]
    if sub_last:
        o_ref[...] = o_ref[...] - off


def _cumsum_lanes(x, sub_last=False):
    tri = (jnp.arange(LANES)[:, None] <= jnp.arange(LANES)[None, :]).astype(BF16)
    return pl.pallas_call(
        functools.partial(_cumsum_kernel, sub_last=sub_last),
        out_shape=jax.ShapeDtypeStruct(x.shape, F32),
        compiler_params=pltpu.CompilerParams(vmem_limit_bytes=VMEM_LIMIT_BYTES),
        name="cumsum_lanes",
    )(x, tri)


def _att_prompt_kernel(q_ref, k_ref, v_ref, dk_ref, tri_ref, o_ref,
                       v0_ref, v1_ref, q0_ref, q1_ref, m_ref, acc_ref):
    hq = pl.program_id(1)
    qi = pl.program_id(2)
    tq = q_ref.shape[0]
    width = q_ref.shape[1]
    pairs = width // LANES
    low = (lax.broadcasted_iota(jnp.int32, (1, width), 1) & (LANES - 1)) < C_HEAD_DIM
    low1 = low[:, 0:LANES]
    low_rows = (lax.broadcasted_iota(jnp.int32, (width, 1), 0) & (LANES - 1)) < C_HEAD_DIM

    @pl.when(qi == 0)
    def _():
        v = v_ref[...]
        one = jnp.ones_like(v)
        v0_ref[...] = jnp.where(low_rows, v, one)
        v1_ref[...] = jnp.where(low_rows, one, v)

    q = q_ref[...]
    zero = jnp.zeros_like(q)
    q0_ref[...] = jnp.where(low, q, zero)
    q1_ref[...] = jnp.where(low, zero, q)
    m_ref[...] = jnp.full(m_ref.shape, NEG_INF, F32)
    acc_ref[...] = jnp.zeros_like(acc_ref)
    qh = (q0_ref, q1_ref)
    vh = (v0_ref, v1_ref)
    nt = (((1,), (1,)), ((), ()))

    def update(c, ks, span, r0, r1, causal):
        pr, e = divmod(c, 2)
        cols = slice(pr * LANES, (pr + 1) * LANES)
        s = jnp.dot(qh[e][r0:r1, cols], k_ref[cols, pl.ds(ks, span)],
                    preferred_element_type=F32)
        s = s - dk_ref[pl.ds(2 * pairs * hq + c, 1), pl.ds(ks, span)]
        if causal:
            band = r1 - r0
            last = s[:, span - band:] + tri_ref[...]
            s = last if span == band else jnp.concatenate([s[:, :span - band], last], axis=1)
        m = m_ref[c, r0:r1, :]
        m_new = jnp.maximum(m, jnp.max(s, axis=-1, keepdims=True))
        alpha = jnp.exp2(m - m_new)
        p = jnp.exp2(s - jnp.concatenate([m_new] * (span // LANES), axis=1)).astype(BF16)
        pv = lax.dot_general(p, vh[e][cols, pl.ds(ks, span)], nt, preferred_element_type=F32)
        acc_ref[c, r0:r1, :] = alpha * acc_ref[c, r0:r1, :] + pv
        m_ref[c, r0:r1, :] = m_new

    per_tile = tq // ATT_TK

    def full_blocks(j, _):
        for i in range(per_tile):
            ks = pl.multiple_of((j * per_tile + i) * ATT_TK, ATT_TK)
            for c in range(2 * pairs):
                update(c, ks, ATT_TK, 0, tq, False)
        return 0

    lax.fori_loop(0, qi, full_blocks, 0)
    band = tri_ref.shape[0]
    ks0 = pl.multiple_of(qi * tq, ATT_TK)
    for i in range(tq // band):
        for c in range(2 * pairs):
            update(c, ks0, (i + 1) * band, i * band, (i + 1) * band, True)
    for pr in range(pairs):
        acc0, acc1 = acc_ref[2 * pr], acc_ref[2 * pr + 1]
        o0 = acc0 * (1.0 / acc0[:, C_HEAD_DIM:C_HEAD_DIM + 1])
        o1 = acc1 * (1.0 / acc1[:, 0:1])
        o_ref[:, pr * LANES:(pr + 1) * LANES] = jnp.where(low1, o0, o1).astype(BF16)


def _att_prompt(q, k, v, dk):
    b, t, _ = q.shape
    tq = min(ATT_TQ, t)
    nq = t // tq
    width = ATT_HEADS * C_HEAD_DIM
    pos = jnp.arange(min(ATT_BAND, tq))
    tri = jnp.where(pos[None, :] <= pos[:, None], 0.0, NEG_INF).astype(F32)
    return pl.pallas_call(
        _att_prompt_kernel,
        grid=(b, C_HEADS // ATT_HEADS, nq),
        in_specs=[pl.BlockSpec((None, tq, width), lambda bi, hi, qi: (bi, qi, hi)),
                  pl.BlockSpec((None, width, t), lambda bi, hi, qi: (bi, hi, 0)),
                  pl.BlockSpec((None, width, t), lambda bi, hi, qi: (bi, hi, 0)),
                  pl.BlockSpec((None, C_HEADS, t), lambda bi, hi, qi: (bi, 0, 0)),
                  _resident(tri.shape)],
        out_specs=pl.BlockSpec((None, tq, width), lambda bi, hi, qi: (bi, qi, hi)),
        out_shape=jax.ShapeDtypeStruct((b, t, D_C), BF16),
        scratch_shapes=[pltpu.VMEM((width, t), BF16), pltpu.VMEM((width, t), BF16),
                        pltpu.VMEM((tq, width), BF16), pltpu.VMEM((tq, width), BF16),
                        pltpu.VMEM((ATT_HEADS, tq, LANES), F32),
                        pltpu.VMEM((ATT_HEADS, tq, LANES), F32)],
        compiler_params=_params(("arbitrary", "arbitrary", "arbitrary")),
        name="att_prompt",
    )(q, k, v, dk, tri)


def _att_sample_kernel(q_ref, kc_ref, vc_ref, kn_ref, vn_ref, dkc_ref, dkn_ref, o_ref,
                       qrows_ref, m_ref, l_ref, acc_ref, *, t_new):
    ci = pl.program_id(1)
    rows = C_HEADS * t_new
    rhead = lax.broadcasted_iota(jnp.int32, (rows, D_C), 0) // t_new
    lhead = lax.broadcasted_iota(jnp.int32, (rows, D_C), 1) // C_HEAD_DIM
    own = rhead == lhead
    nt = (((1,), (1,)), ((), ()))

    def expand(d):
        return jnp.concatenate(
            [jnp.broadcast_to(d[hh:hh + 1], (t_new, d.shape[1])) for hh in range(C_HEADS)], axis=0)

    def wide(x, n):
        return jnp.concatenate([x] * (n // LANES), axis=1)

    @pl.when(ci == 0)
    def _():
        q = q_ref[...]
        qt = jnp.concatenate([q] * C_HEADS, axis=0)
        qrows = jnp.where(own, qt, jnp.zeros_like(qt))
        qrows_ref[...] = qrows
        s_n = lax.dot_general(qrows, kn_ref[...], nt, preferred_element_type=F32)
        s_n = s_n - expand(dkn_ref[...])[:, 0:t_new]
        tpos = lax.broadcasted_iota(jnp.int32, (rows, t_new), 0) & (t_new - 1)
        spos = lax.broadcasted_iota(jnp.int32, (rows, t_new), 1)
        s_n = jnp.where(spos <= tpos, s_n, NEG_INF)
        m0 = jnp.broadcast_to(jnp.max(s_n, axis=-1, keepdims=True), (rows, LANES))
        p_n = jnp.exp2(s_n - m0[:, 0:t_new])
        m_ref[...] = m0
        l_ref[...] = jnp.broadcast_to(jnp.sum(p_n, axis=-1, keepdims=True), (rows, LANES))
        acc_ref[...] = jnp.dot(p_n.astype(BF16), vn_ref[...], preferred_element_type=F32)

    ts = kc_ref.shape[1]
    s = jnp.dot(qrows_ref[...], kc_ref[...].astype(BF16), preferred_element_type=F32)
    s = s - expand(dkc_ref[...])
    m = m_ref[...]
    m_new = jnp.maximum(m, jnp.max(s, axis=-1, keepdims=True))
    alpha = jnp.exp2(m - m_new)
    p = jnp.exp2(s - wide(m_new, ts))
    l_new = alpha * l_ref[...] + jnp.sum(p, axis=-1, keepdims=True)
    acc = wide(alpha, D_C) * acc_ref[...] + lax.dot_general(
        p.astype(BF16), vc_ref[...].astype(BF16), nt, preferred_element_type=F32)
    m_ref[...] = m_new
    l_ref[...] = l_new
    acc_ref[...] = acc

    @pl.when(ci == pl.num_programs(1) - 1)
    def _():
        o_full = jnp.where(own, acc * wide(1.0 / l_new, D_C), 0.0)
        o = o_full[0:t_new]
        for hh in range(1, C_HEADS):
            o = o + o_full[hh * t_new:(hh + 1) * t_new]
        o_ref[...] = o.astype(BF16)


def _att_sample(q, kc, vc, kn, vn, dkc, dkn):
    b, t_new, _ = q.shape
    p_len = kc.shape[2]
    ts = min(SAMPLE_TS, p_len)
    rows = C_HEADS * t_new
    per = lambda bi, ci: (bi, 0, 0)
    cache = pl.BlockSpec((None, D_C, ts), lambda bi, ci: (bi, 0, ci))
    return pl.pallas_call(
        functools.partial(_att_sample_kernel, t_new=t_new),
        grid=(b, p_len // ts),
        in_specs=[pl.BlockSpec((None, t_new, D_C), per), cache, cache,
                  pl.BlockSpec((None, t_new, D_C), per), pl.BlockSpec((None, t_new, D_C), per),
                  pl.BlockSpec((None, C_HEADS, ts), lambda bi, ci: (bi, 0, ci)),
                  pl.BlockSpec((None, C_HEADS, LANES), per)],
        out_specs=pl.BlockSpec((None, t_new, D_C), per),
        out_shape=jax.ShapeDtypeStruct((b, t_new, D_C), BF16),
        scratch_shapes=[pltpu.VMEM((rows, D_C), BF16), pltpu.VMEM((rows, LANES), F32),
                        pltpu.VMEM((rows, LANES), F32), pltpu.VMEM((rows, D_C), F32)],
        compiler_params=_params(("arbitrary", "arbitrary")),
        name="att_sample",
    )(q, kc, vc, kn, vn, dkc, dkn)


def _expand_state(state, t):
    b, _, c = state.shape
    p1 = jnp.zeros((b, t, c), F32).at[:, 0].set(state[:, 1])
    p2 = jnp.zeros((b, t, c), F32).at[:, 0].set(state[:, 0]).at[:, 1].set(state[:, 1])
    return p1.reshape(1, b * t, c), p2.reshape(1, b * t, c)


def kernel(x_prompt, x_sample, state_conv_b, state_ffn, cache_k, cache_v, cache_logf,
           norm_mix, norm_ffn, w_in_ab, sgu_norm, w_spatial, b_spatial, conv_b, w_out_ab,
           w_in_c, b_forget, q_norm, k_norm, w_out_c, w_up, conv_ffn, w_down):
    bp, tp, d = x_prompt.shape
    bs, ts, _ = x_sample.shape
    ns = bs * ts
    xs = x_sample.reshape(1, ns, d)

    pos = jnp.arange(GMLP_CHUNK)
    vis = (pos[None, :] // CHUNK) <= (pos[:, None] // CHUNK)
    w_m = jnp.where(vis[None], w_spatial[0], 0.0)
    wm_p = w_m.astype(BF16)
    bb_p = jnp.broadcast_to(b_spatial[0][:, :, None], (A_GROUPS, GMLP_CHUNK, A_GROUP_DIM))
    reps = GMLP_CHUNK // ts
    eye = jnp.eye(reps, dtype=F32)
    wm_s = jnp.einsum('ab,gts->gatbs', eye, w_m[:, :ts, :ts]).reshape(
        A_GROUPS, GMLP_CHUNK, GMLP_CHUNK).astype(BF16)
    bb_s = jnp.broadcast_to(jnp.tile(b_spatial[0][:, :ts], (1, reps))[:, :, None],
                            (A_GROUPS, GMLP_CHUNK, A_GROUP_DIM))
    g_mix0 = norm_mix[0].reshape(1, d)
    win_ab = w_in_ab[0].astype(BF16)
    sg = sgu_norm[0].reshape(1, D_A)
    cwb = conv_b[0]

    y_p, st_b_p = _mixer_ab(x_prompt, g_mix0, win_ab, sg, wm_p, bb_p, cwb)
    pb1, pb2 = _expand_state(state_conv_b[0], ts)
    y_s, c_s, v_s = _mixer_ab(xs, g_mix0, win_ab, sg, wm_s, bb_s, cwb, seg=ts, p1=pb1, p2=pb2)

    def ffn(layer, x_p, yy_p, x_s, yy_s, w_pre):
        g = norm_ffn[layer].reshape(1, d)
        wup = w_up[layer].astype(BF16)
        wdn = w_down[layer].astype(BF16)
        cw = conv_ffn[layer]
        wpre = w_pre.astype(BF16)
        xo_p, st_p = _post(x_p, yy_p, wpre, g, wup, cw, wdn)
        xo_s, st0_s, st1_s = _post(x_s, yy_s, wpre, g, wup, cw, wdn, seg=ts,
                                   s0=state_ffn[layer, :, 0], s1=state_ffn[layer, :, 1])
        st_s = jnp.stack([st0_s, st1_s], axis=1)
        return xo_p, st_p[:, SUBLANES - (CONV_W - 1):], xo_s, st_s

    xp1, ffn_p0, xs1, ffn_s0 = ffn(0, x_prompt, y_p, xs, y_s, w_out_ab[0])

    g_mix1 = norm_mix[1].reshape(1, d)
    w_c = w_in_c[0]
    wq = w_c[:, 0:D_C].astype(BF16)
    wk = w_c[:, D_C:2 * D_C].astype(BF16)
    wv = w_c[:, 2 * D_C:3 * D_C].astype(BF16)
    wft = w_c[:, 3 * D_C:].T.astype(BF16)
    bf = b_forget[0].reshape(C_HEADS, 1)
    scale = C_HEAD_DIM ** -0.5 * LOG2E
    qg = (jnp.tile(q_norm[0], C_HEADS) * scale).reshape(1, D_C)
    kg = jnp.tile(k_norm[0], C_HEADS)
    kg_rows = jnp.broadcast_to(kg[:, None], (D_C, LANES))

    q_p, kf_p, vf_p, kb_p, vb_p, lf_p = _fox_proj(xp1, g_mix1, wq, wk.T, wv.T, wft, bf, qg,
                                                  kg_rows, transposed=True)
    q_s, kf_s, vf_s, kb_s, vb_s, lf_s = _fox_proj(xs1, g_mix1, wq, wk, wv, wft, bf, qg,
                                                  kg.reshape(1, D_C), transposed=False)

    d_p = _cumsum_lanes(lf_p.reshape(bp * C_HEADS, tp)).reshape(bp, C_HEADS, tp)
    o_p = _att_prompt(q_p, kb_p, vb_p, d_p)

    p_len = cache_k.shape[2]
    lf_s3 = lf_s.reshape(C_HEADS, bs, ts).transpose(1, 0, 2)
    lf_s_pad = jnp.pad(lf_s3, ((0, 0), (0, 0), (0, LANES - ts))).reshape(bs * C_HEADS, LANES)
    dkn = _cumsum_lanes(lf_s_pad).reshape(bs, C_HEADS, LANES)
    lf_c = cache_logf[0].astype(F32).transpose(0, 2, 1).reshape(bs * C_HEADS, p_len)
    dkc = _cumsum_lanes(lf_c, sub_last=True).reshape(bs, C_HEADS, p_len)
    kc = cache_k[0].transpose(0, 2, 3, 1).reshape(bs, D_C, p_len)
    vc = cache_v[0].transpose(0, 2, 3, 1).reshape(bs, D_C, p_len)
    o_s = _att_sample(q_s.reshape(bs, ts, D_C), kc, vc,
                      kb_s.reshape(bs, ts, D_C), vb_s.reshape(bs, ts, D_C), dkc, dkn)

    xp2, ffn_p1, xs2, ffn_s1 = ffn(1, xp1, o_p, xs1, o_s.reshape(1, ns, D_C), w_out_c[0])

    heads = (C_HEADS, C_HEAD_DIM)

    def by_position(xt):
        return xt.reshape(bp, C_HEADS, C_HEAD_DIM, tp).transpose(0, 3, 1, 2)[None]

    return (xp2, xs2.reshape(bs, ts, d),
            st_b_p[None, :, SUBLANES - (CONV_W - 1):],
            c_s.reshape(bs, ts, D_B)[None, :, ts - (CONV_W - 1):],
            v_s.reshape(1, bs, ts, D_A),
            by_position(kf_p), by_position(vf_p),
            lf_p.transpose(0, 2, 1)[None],
            kf_s.reshape(1, bs, ts, *heads), vf_s.reshape(1, bs, ts, *heads),
            lf_s3.transpose(0, 2, 1)[None],
            jnp.stack([ffn_p0, ffn_p1]), jnp.stack([ffn_s0, ffn_s1]))
```

```python
import functools

import jax
import jax.numpy as jnp
from jax import lax
from jax.experimental import pallas as pl
from jax.experimental.pallas import tpu as pltpu

F32 = jnp.float32
BF16 = jnp.bfloat16

D_MODEL = 1024
CHUNK = 64
GMLP_CHUNK = 128
A_GROUP_DIM = 128
D_A = D_MODEL // 2
A_GROUPS = D_A // A_GROUP_DIM
D_B = D_MODEL // 2
CONV_W = 3
C_HEAD_DIM = 64
C_HEADS = D_MODEL // C_HEAD_DIM
D_C = C_HEADS * C_HEAD_DIM
D_FF = 11 * D_MODEL // 4
EPS = 1e-6
NEG_INF = -1e30

LANES = 128
SUBLANES = 8
MXU_COLS = 256
VMEM_LIMIT_BYTES = 56 * 1024 * 1024
ROW_TILE = 1024
FF_CHUNK = MXU_COLS
ATT_TQ = 1024
ATT_HEADS = 4
ATT_TK = MXU_COLS
ATT_BAND = 512
SAMPLE_TS = 2048
LOG2E = 1.4426950408889634


def _params(sem):
    return pltpu.CompilerParams(dimension_semantics=sem, vmem_limit_bytes=VMEM_LIMIT_BYTES)


def _resident(shape):
    nd = len(shape)
    return pl.BlockSpec(shape, lambda *_: (0,) * nd, pipeline_mode=pl.Buffered(1))


def _rms(x, g):
    ms = jnp.mean(x * x, axis=-1, keepdims=True)
    return x * lax.rsqrt(ms + EPS) * g


def _shift_rows_carry(z, carry):
    r1 = pltpu.roll(z, 1, 0)
    r2 = pltpu.roll(z, 2, 0)
    rid = lax.broadcasted_iota(jnp.int32, (SUBLANES, z.shape[1]), 0)
    h1 = jnp.where(rid < 1, pltpu.roll(carry, 1, 0), r1[0:SUBLANES])
    h2 = jnp.where(rid < 2, pltpu.roll(carry, 2, 0), r2[0:SUBLANES])
    z1 = jnp.concatenate([h1, r1[SUBLANES:]], axis=0)
    z2 = jnp.concatenate([h2, r2[SUBLANES:]], axis=0)
    return z1, z2


def _shift_rows_segments(z, p1, p2, seg):
    rid = lax.broadcasted_iota(jnp.int32, z.shape, 0) & (seg - 1)
    z1 = jnp.where(rid >= 1, pltpu.roll(z, 1, 0), p1)
    z2 = jnp.where(rid >= 2, pltpu.roll(z, 2, 0), p2)
    return z1, z2


def _conv3(z, z1, z2, w):
    return z2 * w[0:1] + z1 * w[1:2] + z * w[2:3]


def _ab_kernel(*refs, tm, seg):
    if seg is None:
        (x_ref, g_ref, win_ref, sg_ref, wm_ref, bb_ref, cw_ref,
         y_ref, st_ref, carry_ref) = refs
    else:
        (x_ref, g_ref, win_ref, sg_ref, wm_ref, bb_ref, cw_ref, p1_ref, p2_ref,
         y_ref, c_ref, v_ref) = refs

    h = _rms(x_ref[...], g_ref[...]).astype(BF16)
    z = jnp.dot(h, win_ref[...], preferred_element_type=F32)

    ya = []
    vn = []
    for g in range(A_GROUPS):
        lo = g * A_GROUP_DIM
        u = jax.nn.gelu(z[:, lo:lo + A_GROUP_DIM])
        vg = jax.nn.gelu(z[:, D_A + lo:D_A + lo + A_GROUP_DIM])
        ms = jnp.mean(vg * vg, axis=-1, keepdims=True)
        vg = vg * lax.rsqrt(ms + EPS) * sg_ref[:, lo:lo + A_GROUP_DIM]
        vn.append(vg)
        vb = vg.astype(BF16)
        wm = wm_ref[g]
        bias = bb_ref[g]
        blocks = []
        for c in range(tm // GMLP_CHUNK):
            blk = vb[c * GMLP_CHUNK:(c + 1) * GMLP_CHUNK]
            blocks.append(jnp.dot(wm, blk, preferred_element_type=F32) + bias)
        ya.append(u * jnp.concatenate(blocks, axis=0))

    off = 2 * D_A
    g_b = z[:, off:off + D_B]
    c = z[:, off + D_B:off + 2 * D_B] * z[:, off + 2 * D_B:off + 3 * D_B]
    if seg is None:
        @pl.when(pl.program_id(1) == 0)
        def _():
            carry_ref[...] = jnp.zeros_like(carry_ref)
        c1, c2 = _shift_rows_carry(c, carry_ref[...])
        carry_ref[...] = c[tm - SUBLANES:tm]
        st_ref[...] = c[tm - SUBLANES:tm]
    else:
        c1, c2 = _shift_rows_segments(c, p1_ref[...], p2_ref[...], seg)
        c_ref[...] = c
        v_ref[...] = jnp.concatenate(vn, axis=1)
    yb = g_b * _conv3(c, c1, c2, cw_ref[...])
    y_ref[...] = jnp.concatenate(ya + [yb], axis=1).astype(BF16)


def _mixer_ab(x, g, win, sg, wm, bb, cw, seg=None, p1=None, p2=None):
    b, t, d = x.shape
    tm = min(ROW_TILE, t)
    nt = t // tm
    row = lambda bi, ti: (bi, ti, 0)
    in_specs = [pl.BlockSpec((None, tm, d), row), _resident(g.shape), _resident(win.shape),
                _resident(sg.shape), _resident(wm.shape), _resident(bb.shape), _resident(cw.shape)]
    args = [x, g, win, sg, wm, bb, cw]
    y_shape = jax.ShapeDtypeStruct((b, t, D_A + D_B), BF16)
    y_spec = pl.BlockSpec((None, tm, D_A + D_B), row)
    if seg is None:
        out_shape = (y_shape, jax.ShapeDtypeStruct((b, SUBLANES, D_B), F32))
        out_specs = (y_spec, pl.BlockSpec((None, SUBLANES, D_B), lambda bi, ti: (bi, 0, 0)))
        scratch = [pltpu.VMEM((SUBLANES, D_B), F32)]
    else:
        in_specs += [pl.BlockSpec((None, tm, D_B), row), pl.BlockSpec((None, tm, D_B), row)]
        args += [p1, p2]
        out_shape = (y_shape, jax.ShapeDtypeStruct((b, t, D_B), F32),
                     jax.ShapeDtypeStruct((b, t, D_A), F32))
        out_specs = (y_spec, pl.BlockSpec((None, tm, D_B), row), pl.BlockSpec((None, tm, D_A), row))
        scratch = []
    return pl.pallas_call(
        functools.partial(_ab_kernel, tm=tm, seg=seg),
        grid=(b, nt), in_specs=in_specs, out_specs=out_specs, out_shape=out_shape,
        scratch_shapes=scratch, compiler_params=_params(("arbitrary", "arbitrary")),
        name="mixer_ab_seg" if seg else "mixer_ab",
    )(*args)


def _post_kernel(*refs, tm, seg):
    if seg is None:
        (x_ref, y_ref, wpre_ref, g_ref, wup_ref, cw_ref, wdn_ref,
         xo_ref, st_ref, a_ref, carry_ref) = refs
    else:
        (x_ref, y_ref, wpre_ref, g_ref, wup_ref, cw_ref, wdn_ref, s0_ref, s1_ref,
         xo_ref, st0_ref, st1_ref, a_ref, p1_ref, p2_ref, zs_ref) = refs
        nseg = tm // seg

    x1 = x_ref[...] + jnp.dot(y_ref[...], wpre_ref[...], preferred_element_type=F32)
    h = _rms(x1, g_ref[...]).astype(BF16)

    if seg is None:
        @pl.when(pl.program_id(1) == 0)
        def _():
            carry_ref[...] = jnp.zeros_like(carry_ref)
    else:
        p1_ref[...] = jnp.zeros_like(p1_ref)
        p2_ref[...] = jnp.zeros_like(p2_ref)

    def conv_cols(lo):
        z = jnp.dot(h, wup_ref[:, lo:lo + FF_CHUNK], preferred_element_type=F32)
        if seg is None:
            z1, z2 = _shift_rows_carry(z, carry_ref[:, lo:lo + FF_CHUNK])
            carry_ref[:, lo:lo + FF_CHUNK] = z[tm - SUBLANES:tm]
            st_ref[:, lo:lo + FF_CHUNK] = z[tm - SUBLANES:tm]
        else:
            for i in range(FF_CHUNK // LANES):
                cols = slice(lo + i * LANES, lo + (i + 1) * LANES)
                p1_ref[i, pl.ds(0, nseg, stride=seg), :] = s1_ref[:, cols]
                p2_ref[i, pl.ds(0, nseg, stride=seg), :] = s0_ref[:, cols]
                p2_ref[i, pl.ds(1, nseg, stride=seg), :] = s1_ref[:, cols]
                zs_ref[i] = z[:, i * LANES:(i + 1) * LANES]
                st0_ref[:, cols] = zs_ref[i, pl.ds(seg - 2, nseg, stride=seg), :]
                st1_ref[:, cols] = zs_ref[i, pl.ds(seg - 1, nseg, stride=seg), :]
            p1 = jnp.concatenate([p1_ref[i] for i in range(FF_CHUNK // LANES)], axis=1)
            p2 = jnp.concatenate([p2_ref[i] for i in range(FF_CHUNK // LANES)], axis=1)
            z1, z2 = _shift_rows_segments(z, p1, p2, seg)
        return _conv3(z, z1, z2, cw_ref[:, lo:lo + FF_CHUNK])

    for j in range(D_FF // FF_CHUNK):
        gate = conv_cols(j * FF_CHUNK)
        up = conv_cols(D_FF + j * FF_CHUNK)
        a_ref[:, j * FF_CHUNK:(j + 1) * FF_CHUNK] = (jax.nn.silu(gate) * up).astype(BF16)

    xo_ref[...] = x1 + jnp.dot(a_ref[...], wdn_ref[...], preferred_element_type=F32)


def _post(x, y, wpre, g, wup, cw, wdn, seg=None, s0=None, s1=None):
    b, t, d = x.shape
    tm = min(ROW_TILE, t)
    nt = t // tm
    assert seg is None or (b == 1 and nt == 1), "packed streams must fit one row tile"
    row = lambda bi, ti: (bi, ti, 0)
    in_specs = [pl.BlockSpec((None, tm, d), row), pl.BlockSpec((None, tm, y.shape[2]), row),
                _resident(wpre.shape), _resident(g.shape), _resident(wup.shape),
                _resident(cw.shape), _resident(wdn.shape)]
    args = [x, y, wpre, g, wup, cw, wdn]
    xo_shape = jax.ShapeDtypeStruct((b, t, d), F32)
    xo_spec = pl.BlockSpec((None, tm, d), row)
    scratch = [pltpu.VMEM((tm, D_FF), BF16)]
    if seg is None:
        out_shape = (xo_shape, jax.ShapeDtypeStruct((b, SUBLANES, 2 * D_FF), F32))
        out_specs = (xo_spec, pl.BlockSpec((None, SUBLANES, 2 * D_FF), lambda bi, ti: (bi, 0, 0)))
        scratch.append(pltpu.VMEM((SUBLANES, 2 * D_FF), F32))
    else:
        in_specs += [_resident(s0.shape), _resident(s1.shape)]
        args += [s0, s1]
        st_shape = jax.ShapeDtypeStruct(s0.shape, F32)
        st_spec = pl.BlockSpec(s0.shape, lambda bi, ti: (0, 0))
        out_shape = (xo_shape, st_shape, st_shape)
        out_specs = (xo_spec, st_spec, st_spec)
        scratch += [pltpu.VMEM((FF_CHUNK // LANES, tm, LANES), F32)] * 3
    return pl.pallas_call(
        functools.partial(_post_kernel, tm=tm, seg=seg),
        grid=(b, nt), in_specs=in_specs, out_specs=out_specs, out_shape=out_shape,
        scratch_shapes=scratch, compiler_params=_params(("arbitrary", "arbitrary")),
        name="post_ffn_seg" if seg else "post_ffn",
    )(*args)


def _fox_proj_kernel(x_ref, g_ref, wq_ref, wk_ref, wv_ref, wf_ref, bf_ref, qg_ref, kg_ref,
                     q_ref, kf_ref, vf_ref, kb_ref, vb_ref, lf_ref, *, transposed):
    h = _rms(x_ref[...], g_ref[...]).astype(BF16)
    nt = (((1,), (1,)), ((), ()))
    low = lax.broadcasted_iota(jnp.int32, (1, LANES), 1) < C_HEAD_DIM

    def head_norm(x, gain):
        out = []
        for c in range(D_C // LANES):
            xb = x[:, c * LANES:(c + 1) * LANES]
            sq = xb * xb
            lo = jnp.sum(jnp.where(low, sq, 0.0), axis=-1, keepdims=True)
            hi = jnp.sum(jnp.where(low, 0.0, sq), axis=-1, keepdims=True)
            ms = jnp.where(low, lo, hi) * (1.0 / C_HEAD_DIM)
            out.append(xb * lax.rsqrt(ms + EPS) * gain[:, c * LANES:(c + 1) * LANES])
        return jnp.concatenate(out, axis=1)

    q = jnp.dot(h, wq_ref[...], preferred_element_type=F32)
    q_ref[...] = head_norm(q, qg_ref[...]).astype(BF16)
    if transposed:
        kt = lax.dot_general(wk_ref[...], h, nt, preferred_element_type=F32)
        tm = kt.shape[1]
        k3 = kt.reshape(C_HEADS, C_HEAD_DIM, tm)
        ms = jnp.mean(k3 * k3, axis=1, keepdims=True)
        kn = (k3 * lax.rsqrt(ms + EPS)).reshape(D_C, tm)
        kn = kn * jnp.concatenate([kg_ref[...]] * (tm // LANES), axis=1)
        v = lax.dot_general(wv_ref[...], h, nt, preferred_element_type=F32)
    else:
        kn = head_norm(jnp.dot(h, wk_ref[...], preferred_element_type=F32), kg_ref[...])
        v = jnp.dot(h, wv_ref[...], preferred_element_type=F32)
    kf_ref[...] = kn
    vf_ref[...] = v
    kb_ref[...] = kn.astype(BF16)
    vb_ref[...] = v.astype(BF16)
    f = lax.dot_general(wf_ref[...], h, nt, preferred_element_type=F32) + bf_ref[...]
    lf_ref[...] = jnp.minimum(f, 0.0) - jnp.log1p(jnp.exp(-jnp.abs(f)))


def _fox_proj(x, g, wq, wk, wv, wft, bf, qg, kg, transposed):
    b, t, d = x.shape
    tm = min(ROW_TILE, t)
    row = pl.BlockSpec((None, tm, D_C), lambda bi, ti: (bi, ti, 0))
    if transposed:
        kv = pl.BlockSpec((None, D_C, tm), lambda bi, ti: (bi, 0, ti))
        kv_shape = (b, D_C, t)
    else:
        kv = row
        kv_shape = (b, t, D_C)
    weights = [g, wq, wk, wv, wft, bf, qg, kg]
    return pl.pallas_call(
        functools.partial(_fox_proj_kernel, transposed=transposed),
        grid=(b, t // tm),
        in_specs=[pl.BlockSpec((None, tm, d), lambda bi, ti: (bi, ti, 0))]
        + [_resident(w.shape) for w in weights],
        out_specs=(row, kv, kv, kv, kv,
                   pl.BlockSpec((None, C_HEADS, tm), lambda bi, ti: (bi, 0, ti))),
        out_shape=(jax.ShapeDtypeStruct((b, t, D_C), BF16),
                   jax.ShapeDtypeStruct(kv_shape, F32), jax.ShapeDtypeStruct(kv_shape, F32),
                   jax.ShapeDtypeStruct(kv_shape, BF16), jax.ShapeDtypeStruct(kv_shape, BF16),
                   jax.ShapeDtypeStruct((b, C_HEADS, t), F32)),
        compiler_params=_params(("arbitrary", "arbitrary")),
        name="fox_proj_t" if transposed else "fox_proj",
    )(x, *weights)


def _cumsum_kernel(x_ref, tri_ref, o_ref, *, sub_last):
    r, n = x_ref.shape
    tri = tri_ref[...]
    off = jnp.zeros((r, 1), F32)
    for i in range(n // LANES):
        xb = x_ref[:, i * LANES:(i + 1) * LANES] * LOG2E
        hi = xb.astype(BF16)
        r1 = xb - hi.astype(F32)
        mid = r1.astype(BF16)
        lo = (r1 - mid.astype(F32)).astype(BF16)
        c = (jnp.dot(hi, tri, preferred_element_type=F32)
             + jnp.dot(mid, tri, preferred_element_type=F32)
             + jnp.dot(lo, tri, preferred_element_type=F32)) + off
        o_ref[:, i * LANES:(i + 1) * LANES] = c
        off = c[:, LANES - 1:LANES]
    if sub_last:
        o_ref[...] = o_ref[...] - off


def _cumsum_lanes(x, sub_last=False):
    tri = (jnp.arange(LANES)[:, None] <= jnp.arange(LANES)[None, :]).astype(BF16)
    return pl.pallas_call(
        functools.partial(_cumsum_kernel, sub_last=sub_last),
        out_shape=jax.ShapeDtypeStruct(x.shape, F32),
        compiler_params=pltpu.CompilerParams(vmem_limit_bytes=VMEM_LIMIT_BYTES),
        name="cumsum_lanes",
    )(x, tri)


def _att_prompt_kernel(q_ref, k_ref, v_ref, dk_ref, tri_ref, o_ref,
                       v0_ref, v1_ref, q0_ref, q1_ref, m_ref, acc_ref):
    hq = pl.program_id(1)
    qi = pl.program_id(2)
    tq = q_ref.shape[0]
    width = q_ref.shape[1]
    pairs = width // LANES
    low = (lax.broadcasted_iota(jnp.int32, (1, width), 1) & (LANES - 1)) < C_HEAD_DIM
    low1 = low[:, 0:LANES]
    low_rows = (lax.broadcasted_iota(jnp.int32, (width, 1), 0) & (LANES - 1)) < C_HEAD_DIM

    @pl.when(qi == 0)
    def _():
        v = v_ref[...]
        one = jnp.ones_like(v)
        v0_ref[...] = jnp.where(low_rows, v, one)
        v1_ref[...] = jnp.where(low_rows, one, v)

    q = q_ref[...]
    zero = jnp.zeros_like(q)
    q0_ref[...] = jnp.where(low, q, zero)
    q1_ref[...] = jnp.where(low, zero, q)
    m_ref[...] = jnp.full(m_ref.shape, NEG_INF, F32)
    acc_ref[...] = jnp.zeros_like(acc_ref)
    qh = (q0_ref, q1_ref)
    vh = (v0_ref, v1_ref)
    nt = (((1,), (1,)), ((), ()))

    def update(c, ks, span, r0, r1, causal):
        pr, e = divmod(c, 2)
        cols = slice(pr * LANES, (pr + 1) * LANES)
        s = jnp.dot(qh[e][r0:r1, cols], k_ref[cols, pl.ds(ks, span)],
                    preferred_element_type=F32)
        s = s - dk_ref[pl.ds(2 * pairs * hq + c, 1), pl.ds(ks, span)]
        if causal:
            band = r1 - r0
            last = s[:, span - band:] + tri_ref[...]
            s = last if span == band else jnp.concatenate([s[:, :span - band], last], axis=1)
        m = m_ref[c, r0:r1, :]
        m_new = jnp.maximum(m, jnp.max(s, axis=-1, keepdims=True))
        alpha = jnp.exp2(m - m_new)
        p = jnp.exp2(s - jnp.concatenate([m_new] * (span // LANES), axis=1)).astype(BF16)
        pv = lax.dot_general(p, vh[e][cols, pl.ds(ks, span)], nt, preferred_element_type=F32)
        acc_ref[c, r0:r1, :] = alpha * acc_ref[c, r0:r1, :] + pv
        m_ref[c, r0:r1, :] = m_new

    per_tile = tq // ATT_TK

    def full_blocks(j, _):
        for i in range(per_tile):
            ks = pl.multiple_of((j * per_tile + i) * ATT_TK, ATT_TK)
            for c in range(2 * pairs):
                update(c, ks, ATT_TK, 0, tq, False)
        return 0

    lax.fori_loop(0, qi, full_blocks, 0)
    band = tri_ref.shape[0]
    ks0 = pl.multiple_of(qi * tq, ATT_TK)
    for i in range(tq // band):
        for c in range(2 * pairs):
            update(c, ks0, (i + 1) * band, i * band, (i + 1) * band, True)
    for pr in range(pairs):
        acc0, acc1 = acc_ref[2 * pr], acc_ref[2 * pr + 1]
        o0 = acc0 * (1.0 / acc0[:, C_HEAD_DIM:C_HEAD_DIM + 1])
        o1 = acc1 * (1.0 / acc1[:, 0:1])
        o_ref[:, pr * LANES:(pr + 1) * LANES] = jnp.where(low1, o0, o1).astype(BF16)


def _att_prompt(q, k, v, dk):
    b, t, _ = q.shape
    tq = min(ATT_TQ, t)
    nq = t // tq
    width = ATT_HEADS * C_HEAD_DIM
    pos = jnp.arange(min(ATT_BAND, tq))
    tri = jnp.where(pos[None, :] <= pos[:, None], 0.0, NEG_INF).astype(F32)
    return pl.pallas_call(
        _att_prompt_kernel,
        grid=(b, C_HEADS // ATT_HEADS, nq),
        in_specs=[pl.BlockSpec((None, tq, width), lambda bi, hi, qi: (bi, qi, hi)),
                  pl.BlockSpec((None, width, t), lambda bi, hi, qi: (bi, hi, 0)),
                  pl.BlockSpec((None, width, t), lambda bi, hi, qi: (bi, hi, 0)),
                  pl.BlockSpec((None, C_HEADS, t), lambda bi, hi, qi: (bi, 0, 0)),
                  _resident(tri.shape)],
        out_specs=pl.BlockSpec((None, tq, width), lambda bi, hi, qi: (bi, qi, hi)),
        out_shape=jax.ShapeDtypeStruct((b, t, D_C), BF16),
        scratch_shapes=[pltpu.VMEM((width, t), BF16), pltpu.VMEM((width, t), BF16),
                        pltpu.VMEM((tq, width), BF16), pltpu.VMEM((tq, width), BF16),
                        pltpu.VMEM((ATT_HEADS, tq, LANES), F32),
                        pltpu.VMEM((ATT_HEADS, tq, LANES), F32)],
        compiler_params=_params(("arbitrary", "arbitrary", "arbitrary")),
        name="att_prompt",
    )(q, k, v, dk, tri)


def _att_sample_kernel(q_ref, kc_ref, vc_ref, kn_ref, vn_ref, dkc_ref, dkn_ref, o_ref,
                       qrows_ref, m_ref, l_ref, acc_ref, *, t_new):
    ci = pl.program_id(1)
    rows = C_HEADS * t_new
    rhead = lax.broadcasted_iota(jnp.int32, (rows, D_C), 0) // t_new
    lhead = lax.broadcasted_iota(jnp.int32, (rows, D_C), 1) // C_HEAD_DIM
    own = rhead == lhead
    nt = (((1,), (1,)), ((), ()))

    def expand(d):
        return jnp.concatenate(
            [jnp.broadcast_to(d[hh:hh + 1], (t_new, d.shape[1])) for hh in range(C_HEADS)], axis=0)

    def wide(x, n):
        return jnp.concatenate([x] * (n // LANES), axis=1)

    @pl.when(ci == 0)
    def _():
        q = q_ref[...]
        qt = jnp.concatenate([q] * C_HEADS, axis=0)
        qrows = jnp.where(own, qt, jnp.zeros_like(qt))
        qrows_ref[...] = qrows
        s_n = lax.dot_general(qrows, kn_ref[...], nt, preferred_element_type=F32)
        s_n = s_n - expand(dkn_ref[...])[:, 0:t_new]
        tpos = lax.broadcasted_iota(jnp.int32, (rows, t_new), 0) & (t_new - 1)
        spos = lax.broadcasted_iota(jnp.int32, (rows, t_new), 1)
        s_n = jnp.where(spos <= tpos, s_n, NEG_INF)
        m0 = jnp.broadcast_to(jnp.max(s_n, axis=-1, keepdims=True), (rows, LANES))
        p_n = jnp.exp2(s_n - m0[:, 0:t_new])
        m_ref[...] = m0
        l_ref[...] = jnp.broadcast_to(jnp.sum(p_n, axis=-1, keepdims=True), (rows, LANES))
        acc_ref[...] = jnp.dot(p_n.astype(BF16), vn_ref[...], preferred_element_type=F32)

    ts = kc_ref.shape[1]
    s = jnp.dot(qrows_ref[...], kc_ref[...].astype(BF16), preferred_element_type=F32)
    s = s - expand(dkc_ref[...])
    m = m_ref[...]
    m_new = jnp.maximum(m, jnp.max(s, axis=-1, keepdims=True))
    alpha = jnp.exp2(m - m_new)
    p = jnp.exp2(s - wide(m_new, ts))
    l_new = alpha * l_ref[...] + jnp.sum(p, axis=-1, keepdims=True)
    acc = wide(alpha, D_C) * acc_ref[...] + lax.dot_general(
        p.astype(BF16), vc_ref[...].astype(BF16), nt, preferred_element_type=F32)
    m_ref[...] = m_new
    l_ref[...] = l_new
    acc_ref[...] = acc

    @pl.when(ci == pl.num_programs(1) - 1)
    def _():
        o_full = jnp.where(own, acc * wide(1.0 / l_new, D_C), 0.0)
        o = o_full[0:t_new]
        for hh in range(1, C_HEADS):
            o = o + o_full[hh * t_new:(hh + 1) * t_new]
        o_ref[...] = o.astype(BF16)


def _att_sample(q, kc, vc, kn, vn, dkc, dkn):
    b, t_new, _ = q.shape
    p_len = kc.shape[2]
    ts = min(SAMPLE_TS, p_len)
    rows = C_HEADS * t_new
    per = lambda bi, ci: (bi, 0, 0)
    cache = pl.BlockSpec((None, D_C, ts), lambda bi, ci: (bi, 0, ci))
    return pl.pallas_call(
        functools.partial(_att_sample_kernel, t_new=t_new),
        grid=(b, p_len // ts),
        in_specs=[pl.BlockSpec((None, t_new, D_C), per), cache, cache,
                  pl.BlockSpec((None, t_new, D_C), per), pl.BlockSpec((None, t_new, D_C), per),
                  pl.BlockSpec((None, C_HEADS, ts), lambda bi, ci: (bi, 0, ci)),
                  pl.BlockSpec((None, C_HEADS, LANES), per)],
        out_specs=pl.BlockSpec((None, t_new, D_C), per),
        out_shape=jax.ShapeDtypeStruct((b, t_new, D_C), BF16),
        scratch_shapes=[pltpu.VMEM((rows, D_C), BF16), pltpu.VMEM((rows, LANES), F32),
                        pltpu.VMEM((rows, LANES), F32), pltpu.VMEM((rows, D_C), F32)],
        compiler_params=_params(("arbitrary", "arbitrary")),
        name="att_sample",
    )(q, kc, vc, kn, vn, dkc, dkn)


def _expand_state(state, t):
    b, _, c = state.shape
    p1 = jnp.zeros((b, t, c), F32).at[:, 0].set(state[:, 1])
    p2 = jnp.zeros((b, t, c), F32).at[:, 0].set(state[:, 0]).at[:, 1].set(state[:, 1])
    return p1.reshape(1, b * t, c), p2.reshape(1, b * t, c)


def kernel(x_prompt, x_sample, state_conv_b, state_ffn, cache_k, cache_v, cache_logf,
           norm_mix, norm_ffn, w_in_ab, sgu_norm, w_spatial, b_spatial, conv_b, w_out_ab,
           w_in_c, b_forget, q_norm, k_norm, w_out_c, w_up, conv_ffn, w_down):
    bp, tp, d = x_prompt.shape
    bs, ts, _ = x_sample.shape
    ns = bs * ts
    xs = x_sample.reshape(1, ns, d)

    pos = jnp.arange(GMLP_CHUNK)
    vis = (pos[None, :] // CHUNK) <= (pos[:, None] // CHUNK)
    w_m = jnp.where(vis[None], w_spatial[0], 0.0)
    wm_p = w_m.astype(BF16)
    bb_p = jnp.broadcast_to(b_spatial[0][:, :, None], (A_GROUPS, GMLP_CHUNK, A_GROUP_DIM))
    reps = GMLP_CHUNK // ts
    eye = jnp.eye(reps, dtype=F32)
    wm_s = jnp.einsum('ab,gts->gatbs', eye, w_m[:, :ts, :ts]).reshape(
        A_GROUPS, GMLP_CHUNK, GMLP_CHUNK).astype(BF16)
    bb_s = jnp.broadcast_to(jnp.tile(b_spatial[0][:, :ts], (1, reps))[:, :, None],
                            (A_GROUPS, GMLP_CHUNK, A_GROUP_DIM))
    g_mix0 = norm_mix[0].reshape(1, d)
    win_ab = w_in_ab[0].astype(BF16)
    sg = sgu_norm[0].reshape(1, D_A)
    cwb = conv_b[0]

    y_p, st_b_p = _mixer_ab(x_prompt, g_mix0, win_ab, sg, wm_p, bb_p, cwb)
    pb1, pb2 = _expand_state(state_conv_b[0], ts)
    y_s, c_s, v_s = _mixer_ab(xs, g_mix0, win_ab, sg, wm_s, bb_s, cwb, seg=ts, p1=pb1, p2=pb2)

    def ffn(layer, x_p, yy_p, x_s, yy_s, w_pre):
        g = norm_ffn[layer].reshape(1, d)
        wup = w_up[layer].astype(BF16)
        wdn = w_down[layer].astype(BF16)
        cw = conv_ffn[layer]
        wpre = w_pre.astype(BF16)
        xo_p, st_p = _post(x_p, yy_p, wpre, g, wup, cw, wdn)
        xo_s, st0_s, st1_s = _post(x_s, yy_s, wpre, g, wup, cw, wdn, seg=ts,
                                   s0=state_ffn[layer, :, 0], s1=state_ffn[layer, :, 1])
        st_s = jnp.stack([st0_s, st1_s], axis=1)
        return xo_p, st_p[:, SUBLANES - (CONV_W - 1):], xo_s, st_s

    xp1, ffn_p0, xs1, ffn_s0 = ffn(0, x_prompt, y_p, xs, y_s, w_out_ab[0])

    g_mix1 = norm_mix[1].reshape(1, d)
    w_c = w_in_c[0]
    wq = w_c[:, 0:D_C].astype(BF16)
    wk = w_c[:, D_C:2 * D_C].astype(BF16)
    wv = w_c[:, 2 * D_C:3 * D_C].astype(BF16)
    wft = w_c[:, 3 * D_C:].T.astype(BF16)
    bf = b_forget[0].reshape(C_HEADS, 1)
    scale = C_HEAD_DIM ** -0.5 * LOG2E
    qg = (jnp.tile(q_norm[0], C_HEADS) * scale).reshape(1, D_C)
    kg = jnp.tile(k_norm[0], C_HEADS)
    kg_rows = jnp.broadcast_to(kg[:, None], (D_C, LANES))

    q_p, kf_p, vf_p, kb_p, vb_p, lf_p = _fox_proj(xp1, g_mix1, wq, wk.T, wv.T, wft, bf, qg,
                                                  kg_rows, transposed=True)
    q_s, kf_s, vf_s, kb_s, vb_s, lf_s = _fox_proj(xs1, g_mix1, wq, wk, wv, wft, bf, qg,
                                                  kg.reshape(1, D_C), transposed=False)

    d_p = _cumsum_lanes(lf_p.reshape(bp * C_HEADS, tp)).reshape(bp, C_HEADS, tp)
    o_p = _att_prompt(q_p, kb_p, vb_p, d_p)

    p_len = cache_k.shape[2]
    lf_s3 = lf_s.reshape(C_HEADS, bs, ts).transpose(1, 0, 2)
    lf_s_pad = jnp.pad(lf_s3, ((0, 0), (0, 0), (0, LANES - ts))).reshape(bs * C_HEADS, LANES)
    dkn = _cumsum_lanes(lf_s_pad).reshape(bs, C_HEADS, LANES)
    lf_c = cache_logf[0].astype(F32).transpose(0, 2, 1).reshape(bs * C_HEADS, p_len)
    dkc = _cumsum_lanes(lf_c, sub_last=True).reshape(bs, C_HEADS, p_len)
    kc = cache_k[0].transpose(0, 2, 3, 1).reshape(bs, D_C, p_len)
    vc = cache_v[0].transpose(0, 2, 3, 1).reshape(bs, D_C, p_len)
    o_s = _att_sample(q_s.reshape(bs, ts, D_C), kc, vc,
                      kb_s.reshape(bs, ts, D_C), vb_s.reshape(bs, ts, D_C), dkc, dkn)

    xp2, ffn_p1, xs2, ffn_s1 = ffn(1, xp1, o_p, xs1, o_s.reshape(1, ns, D_C), w_out_c[0])

    heads = (C_HEADS, C_HEAD_DIM)

    def by_position(xt):
        return xt.reshape(bp, C_HEADS, C_HEAD_DIM, tp).transpose(0, 3, 1, 2)[None]

    return (xp2, xs2.reshape(bs, ts, d),
            st_b_p[None, :, SUBLANES - (CONV_W - 1):],
            c_s.reshape(bs, ts, D_B)[None, :, ts - (CONV_W - 1):],
            v_s.reshape(1, bs, ts, D_A),
            by_position(kf_p), by_position(vf_p),
            lf_p.transpose(0, 2, 1)[None],
            kf_s.reshape(1, bs, ts, *heads), vf_s.reshape(1, bs, ts, *heads),
            lf_s3.transpose(0, 2, 1)[None],
            jnp.stack([ffn_p0, ffn_p1]), jnp.stack([ffn_s0, ffn_s1]))
```

```python
import functools

import jax
import jax.numpy as jnp
from jax import lax
from jax.experimental import pallas as pl
from jax.experimental.pallas import tpu as pltpu

F32 = jnp.float32
BF16 = jnp.bfloat16

D_MODEL = 1024
CHUNK = 64
GMLP_CHUNK = 128
A_GROUP_DIM = 128
D_A = D_MODEL // 2
A_GROUPS = D_A // A_GROUP_DIM
D_B = D_MODEL // 2
CONV_W = 3
C_HEAD_DIM = 64
C_HEADS = D_MODEL // C_HEAD_DIM
D_C = C_HEADS * C_HEAD_DIM
D_FF = 11 * D_MODEL // 4
EPS = 1e-6
NEG_INF = -1e30

LANES = 128
SUBLANES = 8
MXU_COLS = 256
VMEM_LIMIT_BYTES = 56 * 1024 * 1024
ROW_TILE = 1024
FF_CHUNK = MXU_COLS
ATT_TQ = 1024
ATT_HEADS = 4
ATT_TK = MXU_COLS
ATT_BAND = 512
SAMPLE_TS = 2048
LOG2E = 1.4426950408889634


def _params(sem):
    return pltpu.CompilerParams(dimension_semantics=sem, vmem_limit_bytes=VMEM_LIMIT_BYTES)


def _resident(shape):
    nd = len(shape)
    return pl.BlockSpec(shape, lambda *_: (0,) * nd, pipeline_mode=pl.Buffered(1))


def _rms(x, g):
    ms = jnp.mean(x * x, axis=-1, keepdims=True)
    return x * lax.rsqrt(ms + EPS) * g


def _shift_rows_carry(z, carry):
    r1 = pltpu.roll(z, 1, 0)
    r2 = pltpu.roll(z, 2, 0)
    rid = lax.broadcasted_iota(jnp.int32, (SUBLANES, z.shape[1]), 0)
    h1 = jnp.where(rid < 1, pltpu.roll(carry, 1, 0), r1[0:SUBLANES])
    h2 = jnp.where(rid < 2, pltpu.roll(carry, 2, 0), r2[0:SUBLANES])
    z1 = jnp.concatenate([h1, r1[SUBLANES:]], axis=0)
    z2 = jnp.concatenate([h2, r2[SUBLANES:]], axis=0)
    return z1, z2


def _shift_rows_segments(z, p1, p2, seg):
    rid = lax.broadcasted_iota(jnp.int32, z.shape, 0) & (seg - 1)
    z1 = jnp.where(rid >= 1, pltpu.roll(z, 1, 0), p1)
    z2 = jnp.where(rid >= 2, pltpu.roll(z, 2, 0), p2)
    return z1, z2


def _conv3(z, z1, z2, w):
    return z2 * w[0:1] + z1 * w[1:2] + z * w[2:3]


def _ab_kernel(*refs, tm, seg):
    if seg is None:
        (x_ref, g_ref, win_ref, sg_ref, wm_ref, bb_ref, cw_ref,
         y_ref, st_ref, carry_ref) = refs
    else:
        (x_ref, g_ref, win_ref, sg_ref, wm_ref, bb_ref, cw_ref, p1_ref, p2_ref,
         y_ref, c_ref, v_ref) = refs

    h = _rms(x_ref[...], g_ref[...]).astype(BF16)
    z = jnp.dot(h, win_ref[...], preferred_element_type=F32)

    ya = []
    vn = []
    for g in range(A_GROUPS):
        lo = g * A_GROUP_DIM
        u = jax.nn.gelu(z[:, lo:lo + A_GROUP_DIM])
        vg = jax.nn.gelu(z[:, D_A + lo:D_A + lo + A_GROUP_DIM])
        ms = jnp.mean(vg * vg, axis=-1, keepdims=True)
        vg = vg * lax.rsqrt(ms + EPS) * sg_ref[:, lo:lo + A_GROUP_DIM]
        vn.append(vg)
        vb = vg.astype(BF16)
        wm = wm_ref[g]
        bias = bb_ref[g]
        blocks = []
        for c in range(tm // GMLP_CHUNK):
            blk = vb[c * GMLP_CHUNK:(c + 1) * GMLP_CHUNK]
            blocks.append(jnp.dot(wm, blk, preferred_element_type=F32) + bias)
        ya.append(u * jnp.concatenate(blocks, axis=0))

    off = 2 * D_A
    g_b = z[:, off:off + D_B]
    c = z[:, off + D_B:off + 2 * D_B] * z[:, off + 2 * D_B:off + 3 * D_B]
    if seg is None:
        @pl.when(pl.program_id(1) == 0)
        def _():
            carry_ref[...] = jnp.zeros_like(carry_ref)
        c1, c2 = _shift_rows_carry(c, carry_ref[...])
        carry_ref[...] = c[tm - SUBLANES:tm]
        st_ref[...] = c[tm - SUBLANES:tm]
    else:
        c1, c2 = _shift_rows_segments(c, p1_ref[...], p2_ref[...], seg)
        c_ref[...] = c
        v_ref[...] = jnp.concatenate(vn, axis=1)
    yb = g_b * _conv3(c, c1, c2, cw_ref[...])
    y_ref[...] = jnp.concatenate(ya + [yb], axis=1).astype(BF16)


def _mixer_ab(x, g, win, sg, wm, bb, cw, seg=None, p1=None, p2=None):
    b, t, d = x.shape
    tm = min(ROW_TILE, t)
    nt = t // tm
    row = lambda bi, ti: (bi, ti, 0)
    in_specs = [pl.BlockSpec((None, tm, d), row), _resident(g.shape), _resident(win.shape),
                _resident(sg.shape), _resident(wm.shape), _resident(bb.shape), _resident(cw.shape)]
    args = [x, g, win, sg, wm, bb, cw]
    y_shape = jax.ShapeDtypeStruct((b, t, D_A + D_B), BF16)
    y_spec = pl.BlockSpec((None, tm, D_A + D_B), row)
    if seg is None:
        out_shape = (y_shape, jax.ShapeDtypeStruct((b, SUBLANES, D_B), F32))
        out_specs = (y_spec, pl.BlockSpec((None, SUBLANES, D_B), lambda bi, ti: (bi, 0, 0)))
        scratch = [pltpu.VMEM((SUBLANES, D_B), F32)]
    else:
        in_specs += [pl.BlockSpec((None, tm, D_B), row), pl.BlockSpec((None, tm, D_B), row)]
        args += [p1, p2]
        out_shape = (y_shape, jax.ShapeDtypeStruct((b, t, D_B), F32),
                     jax.ShapeDtypeStruct((b, t, D_A), F32))
        out_specs = (y_spec, pl.BlockSpec((None, tm, D_B), row), pl.BlockSpec((None, tm, D_A), row))
        scratch = []
    return pl.pallas_call(
        functools.partial(_ab_kernel, tm=tm, seg=seg),
        grid=(b, nt), in_specs=in_specs, out_specs=out_specs, out_shape=out_shape,
        scratch_shapes=scratch, compiler_params=_params(("arbitrary", "arbitrary")),
        name="mixer_ab_seg" if seg else "mixer_ab",
    )(*args)


def _post_kernel(*refs, tm, seg):
    if seg is None:
        (x_ref, y_ref, wpre_ref, g_ref, wup_ref, cw_ref, wdn_ref,
         xo_ref, st_ref, a_ref, carry_ref) = refs
    else:
        (x_ref, y_ref, wpre_ref, g_ref, wup_ref, cw_ref, wdn_ref, s0_ref, s1_ref,
         xo_ref, st0_ref, st1_ref, a_ref, p1_ref, p2_ref, zs_ref) = refs
        nseg = tm // seg

    x1 = x_ref[...] + jnp.dot(y_ref[...], wpre_ref[...], preferred_element_type=F32)
    h = _rms(x1, g_ref[...]).astype(BF16)

    if seg is None:
        @pl.when(pl.program_id(1) == 0)
        def _():
            carry_ref[...] = jnp.zeros_like(carry_ref)
    else:
        p1_ref[...] = jnp.zeros_like(p1_ref)
        p2_ref[...] = jnp.zeros_like(p2_ref)

    def conv_cols(lo):
        z = jnp.dot(h, wup_ref[:, lo:lo + FF_CHUNK], preferred_element_type=F32)
        if seg is None:
            z1, z2 = _shift_rows_carry(z, carry_ref[:, lo:lo + FF_CHUNK])
            carry_ref[:, lo:lo + FF_CHUNK] = z[tm - SUBLANES:tm]
            st_ref[:, lo:lo + FF_CHUNK] = z[tm - SUBLANES:tm]
        else:
            for i in range(FF_CHUNK // LANES):
                cols = slice(lo + i * LANES, lo + (i + 1) * LANES)
                p1_ref[i, pl.ds(0, nseg, stride=seg), :] = s1_ref[:, cols]
                p2_ref[i, pl.ds(0, nseg, stride=seg), :] = s0_ref[:, cols]
                p2_ref[i, pl.ds(1, nseg, stride=seg), :] = s1_ref[:, cols]
                zs_ref[i] = z[:, i * LANES:(i + 1) * LANES]
                st0_ref[:, cols] = zs_ref[i, pl.ds(seg - 2, nseg, stride=seg), :]
                st1_ref[:, cols] = zs_ref[i, pl.ds(seg - 1, nseg, stride=seg), :]
            p1 = jnp.concatenate([p1_ref[i] for i in range(FF_CHUNK // LANES)], axis=1)
            p2 = jnp.concatenate([p2_ref[i] for i in range(FF_CHUNK // LANES)], axis=1)
            z1, z2 = _shift_rows_segments(z, p1, p2, seg)
        return _conv3(z, z1, z2, cw_ref[:, lo:lo + FF_CHUNK])

    for j in range(D_FF // FF_CHUNK):
        gate = conv_cols(j * FF_CHUNK)
        up = conv_cols(D_FF + j * FF_CHUNK)
        a_ref[:, j * FF_CHUNK:(j + 1) * FF_CHUNK] = (jax.nn.silu(gate) * up).astype(BF16)

    xo_ref[...] = x1 + jnp.dot(a_ref[...], wdn_ref[...], preferred_element_type=F32)


def _post(x, y, wpre, g, wup, cw, wdn, seg=None, s0=None, s1=None):
    b, t, d = x.shape
    tm = min(ROW_TILE, t)
    nt = t // tm
    assert seg is None or (b == 1 and nt == 1), "packed streams must fit one row tile"
    row = lambda bi, ti: (bi, ti, 0)
    in_specs = [pl.BlockSpec((None, tm, d), row), pl.BlockSpec((None, tm, y.shape[2]), row),
                _resident(wpre.shape), _resident(g.shape), _resident(wup.shape),
                _resident(cw.shape), _resident(wdn.shape)]
    args = [x, y, wpre, g, wup, cw, wdn]
    xo_shape = jax.ShapeDtypeStruct((b, t, d), F32)
    xo_spec = pl.BlockSpec((None, tm, d), row)
    scratch = [pltpu.VMEM((tm, D_FF), BF16)]
    if seg is None:
        out_shape = (xo_shape, jax.ShapeDtypeStruct((b, SUBLANES, 2 * D_FF), F32))
        out_specs = (xo_spec, pl.BlockSpec((None, SUBLANES, 2 * D_FF), lambda bi, ti: (bi, 0, 0)))
        scratch.append(pltpu.VMEM((SUBLANES, 2 * D_FF), F32))
    else:
        in_specs += [_resident(s0.shape), _resident(s1.shape)]
        args += [s0, s1]
        st_shape = jax.ShapeDtypeStruct(s0.shape, F32)
        st_spec = pl.BlockSpec(s0.shape, lambda bi, ti: (0, 0))
        out_shape = (xo_shape, st_shape, st_shape)
        out_specs = (xo_spec, st_spec, st_spec)
        scratch += [pltpu.VMEM((FF_CHUNK // LANES, tm, LANES), F32)] * 3
    return pl.pallas_call(
        functools.partial(_post_kernel, tm=tm, seg=seg),
        grid=(b, nt), in_specs=in_specs, out_specs=out_specs, out_shape=out_shape,
        scratch_shapes=scratch, compiler_params=_params(("arbitrary", "arbitrary")),
        name="post_ffn_seg" if seg else "post_ffn",
    )(*args)


def _fox_proj_kernel(x_ref, g_ref, wq_ref, wk_ref, wv_ref, wf_ref, bf_ref, qg_ref, kg_ref,
                     q_ref, kf_ref, vf_ref, kb_ref, vb_ref, lf_ref, *, transposed):
    h = _rms(x_ref[...], g_ref[...]).astype(BF16)
    nt = (((1,), (1,)), ((), ()))
    low = lax.broadcasted_iota(jnp.int32, (1, LANES), 1) < C_HEAD_DIM

    def head_norm(x, gain):
        out = []
        for c in range(D_C // LANES):
            xb = x[:, c * LANES:(c + 1) * LANES]
            sq = xb * xb
            lo = jnp.sum(jnp.where(low, sq, 0.0), axis=-1, keepdims=True)
            hi = jnp.sum(jnp.where(low, 0.0, sq), axis=-1, keepdims=True)
            ms = jnp.where(low, lo, hi) * (1.0 / C_HEAD_DIM)
            out.append(xb * lax.rsqrt(ms + EPS) * gain[:, c * LANES:(c + 1) * LANES])
        return jnp.concatenate(out, axis=1)

    q = jnp.dot(h, wq_ref[...], preferred_element_type=F32)
    q_ref[...] = head_norm(q, qg_ref[...]).astype(BF16)
    if transposed:
        kt = lax.dot_general(wk_ref[...], h, nt, preferred_element_type=F32)
        tm = kt.shape[1]
        k3 = kt.reshape(C_HEADS, C_HEAD_DIM, tm)
        ms = jnp.mean(k3 * k3, axis=1, keepdims=True)
        kn = (k3 * lax.rsqrt(ms + EPS)).reshape(D_C, tm)
        kn = kn * jnp.concatenate([kg_ref[...]] * (tm // LANES), axis=1)
        v = lax.dot_general(wv_ref[...], h, nt, preferred_element_type=F32)
    else:
        kn = head_norm(jnp.dot(h, wk_ref[...], preferred_element_type=F32), kg_ref[...])
        v = jnp.dot(h, wv_ref[...], preferred_element_type=F32)
    kf_ref[...] = kn
    vf_ref[...] = v
    kb_ref[...] = kn.astype(BF16)
    vb_ref[...] = v.astype(BF16)
    f = lax.dot_general(wf_ref[...], h, nt, preferred_element_type=F32) + bf_ref[...]
    lf_ref[...] = jnp.minimum(f, 0.0) - jnp.log1p(jnp.exp(-jnp.abs(f)))


def _fox_proj(x, g, wq, wk, wv, wft, bf, qg, kg, transposed):
    b, t, d = x.shape
    tm = min(ROW_TILE, t)
    row = pl.BlockSpec((None, tm, D_C), lambda bi, ti: (bi, ti, 0))
    if transposed:
        kv = pl.BlockSpec((None, D_C, tm), lambda bi, ti: (bi, 0, ti))
        kv_shape = (b, D_C, t)
    else:
        kv = row
        kv_shape = (b, t, D_C)
    weights = [g, wq, wk, wv, wft, bf, qg, kg]
    return pl.pallas_call(
        functools.partial(_fox_proj_kernel, transposed=transposed),
        grid=(b, t // tm),
        in_specs=[pl.BlockSpec((None, tm, d), lambda bi, ti: (bi, ti, 0))]
        + [_resident(w.shape) for w in weights],
        out_specs=(row, kv, kv, kv, kv,
                   pl.BlockSpec((None, C_HEADS, tm), lambda bi, ti: (bi, 0, ti))),
        out_shape=(jax.ShapeDtypeStruct((b, t, D_C), BF16),
                   jax.ShapeDtypeStruct(kv_shape, F32), jax.ShapeDtypeStruct(kv_shape, F32),
                   jax.ShapeDtypeStruct(kv_shape, BF16), jax.ShapeDtypeStruct(kv_shape, BF16),
                   jax.ShapeDtypeStruct((b, C_HEADS, t), F32)),
        compiler_params=_params(("arbitrary", "arbitrary")),
        name="fox_proj_t" if transposed else "fox_proj",
    )(x, *weights)


def _cumsum_kernel(x_ref, tri_ref, o_ref, *, sub_last):
    r, n = x_ref.shape
    tri = tri_ref[...]
    off = jnp.zeros((r, 1), F32)
    for i in range(n // LANES):
        xb = x_ref[:, i * LANES:(i + 1) * LANES] * LOG2E
        hi = xb.astype(BF16)
        r1 = xb - hi.astype(F32)
        mid = r1.astype(BF16)
        lo = (r1 - mid.astype(F32)).astype(BF16)
        c = (jnp.dot(hi, tri, preferred_element_type=F32)
             + jnp.dot(mid, tri, preferred_element_type=F32)
             + jnp.dot(lo, tri, preferred_element_type=F32)) + off
        o_ref[:, i * LANES:(i + 1) * LANES] = c
        off = c[:, LANES - 1:LANES]
    if sub_last:
        o_ref[...] = o_ref[...] - off


def _cumsum_lanes(x, sub_last=False):
    tri = (jnp.arange(LANES)[:, None] <= jnp.arange(LANES)[None, :]).astype(BF16)
    return pl.pallas_call(
        functools.partial(_cumsum_kernel, sub_last=sub_last),
        out_shape=jax.ShapeDtypeStruct(x.shape, F32),
        compiler_params=pltpu.CompilerParams(vmem_limit_bytes=VMEM_LIMIT_BYTES),
        name="cumsum_lanes",
    )(x, tri)


def _att_prompt_kernel(q_ref, k_ref, v_ref, dk_ref, tri_ref, o_ref,
                       v0_ref, v1_ref, q0_ref, q1_ref, m_ref, acc_ref):
    hq = pl.program_id(1)
    qi = pl.program_id(2)
    tq = q_ref.shape[0]
    width = q_ref.shape[1]
    pairs = width // LANES
    low = (lax.broadcasted_iota(jnp.int32, (1, width), 1) & (LANES - 1)) < C_HEAD_DIM
    low1 = low[:, 0:LANES]
    low_rows = (lax.broadcasted_iota(jnp.int32, (width, 1), 0) & (LANES - 1)) < C_HEAD_DIM

    @pl.when(qi == 0)
    def _():
        v = v_ref[...]
        one = jnp.ones_like(v)
        v0_ref[...] = jnp.where(low_rows, v, one)
        v1_ref[...] = jnp.where(low_rows, one, v)

    q = q_ref[...]
    zero = jnp.zeros_like(q)
    q0_ref[...] = jnp.where(low, q, zero)
    q1_ref[...] = jnp.where(low, zero, q)
    m_ref[...] = jnp.full(m_ref.shape, NEG_INF, F32)
    acc_ref[...] = jnp.zeros_like(acc_ref)
    qh = (q0_ref, q1_ref)
    vh = (v0_ref, v1_ref)
    nt = (((1,), (1,)), ((), ()))

    def scores(c, ks, span, r0, r1, causal):
        pr, e = divmod(c, 2)
        cols = slice(pr * LANES, (pr + 1) * LANES)
        s = jnp.dot(qh[e][r0:r1, cols], k_ref[cols, pl.ds(ks, span)],
                    preferred_element_type=F32)
        s = s - dk_ref[pl.ds(2 * pairs * hq + c, 1), pl.ds(ks, span)]
        if causal:
            band = r1 - r0
            last = s[:, span - band:] + tri_ref[...]
            s = last if span == band else jnp.concatenate([s[:, :span - band], last], axis=1)
        return s

    def absorb(c, ks, span, r0, r1, s):
        pr, e = divmod(c, 2)
        cols = slice(pr * LANES, (pr + 1) * LANES)
        m = m_ref[c, r0:r1, :]
        m_new = jnp.maximum(m, jnp.max(s, axis=-1, keepdims=True))
        alpha = jnp.exp2(m - m_new)
        p = jnp.exp2(s - jnp.concatenate([m_new] * (span // LANES), axis=1)).astype(BF16)
        pv = lax.dot_general(p, vh[e][cols, pl.ds(ks, span)], nt, preferred_element_type=F32)
        acc_ref[c, r0:r1, :] = alpha * acc_ref[c, r0:r1, :] + pv
        m_ref[c, r0:r1, :] = m_new

    per_tile = tq // ATT_TK

    def full_blocks(j, _):
        for i in range(per_tile):
            ks = pl.multiple_of((j * per_tile + i) * ATT_TK, ATT_TK)
            for c in range(2 * pairs):
                absorb(c, ks, ATT_TK, 0, tq, scores(c, ks, ATT_TK, 0, tq, False))
        return 0

    lax.fori_loop(0, qi, full_blocks, 0)
    band = tri_ref.shape[0]
    ks0 = pl.multiple_of(qi * tq, ATT_TK)
    steps = [(c, ks0, (i + 1) * band, i * band, (i + 1) * band)
             for i in range(tq // band) for c in range(2 * pairs)]
    ahead = scores(*steps[0], True)
    for n, step in enumerate(steps):
        s = ahead
        if n + 1 < len(steps):
            ahead = scores(*steps[n + 1], True)
        absorb(*step, s)
    for pr in range(pairs):
        acc0, acc1 = acc_ref[2 * pr], acc_ref[2 * pr + 1]
        o0 = acc0 * (1.0 / acc0[:, C_HEAD_DIM:C_HEAD_DIM + 1])
        o1 = acc1 * (1.0 / acc1[:, 0:1])
        o_ref[:, pr * LANES:(pr + 1) * LANES] = jnp.where(low1, o0, o1).astype(BF16)


def _att_prompt(q, k, v, dk):
    b, t, _ = q.shape
    tq = min(ATT_TQ, t)
    nq = t // tq
    width = ATT_HEADS * C_HEAD_DIM
    pos = jnp.arange(min(ATT_BAND, tq))
    tri = jnp.where(pos[None, :] <= pos[:, None], 0.0, NEG_INF).astype(F32)
    return pl.pallas_call(
        _att_prompt_kernel,
        grid=(b, C_HEADS // ATT_HEADS, nq),
        in_specs=[pl.BlockSpec((None, tq, width), lambda bi, hi, qi: (bi, qi, hi)),
                  pl.BlockSpec((None, width, t), lambda bi, hi, qi: (bi, hi, 0)),
                  pl.BlockSpec((None, width, t), lambda bi, hi, qi: (bi, hi, 0)),
                  pl.BlockSpec((None, C_HEADS, t), lambda bi, hi, qi: (bi, 0, 0)),
                  _resident(tri.shape)],
        out_specs=pl.BlockSpec((None, tq, width), lambda bi, hi, qi: (bi, qi, hi)),
        out_shape=jax.ShapeDtypeStruct((b, t, D_C), BF16),
        scratch_shapes=[pltpu.VMEM((width, t), BF16), pltpu.VMEM((width, t), BF16),
                        pltpu.VMEM((tq, width), BF16), pltpu.VMEM((tq, width), BF16),
                        pltpu.VMEM((ATT_HEADS, tq, LANES), F32),
                        pltpu.VMEM((ATT_HEADS, tq, LANES), F32)],
        compiler_params=_params(("arbitrary", "arbitrary", "arbitrary")),
        name="att_prompt",
    )(q, k, v, dk, tri)


def _att_sample_kernel(q_ref, kc_ref, vc_ref, kn_ref, vn_ref, dkc_ref, dkn_ref, o_ref,
                       qrows_ref, m_ref, l_ref, acc_ref, *, t_new):
    ci = pl.program_id(1)
    rows = C_HEADS * t_new
    rhead = lax.broadcasted_iota(jnp.int32, (rows, D_C), 0) // t_new
    lhead = lax.broadcasted_iota(jnp.int32, (rows, D_C), 1) // C_HEAD_DIM
    own = rhead == lhead
    nt = (((1,), (1,)), ((), ()))

    def expand(d):
        return jnp.concatenate(
            [jnp.broadcast_to(d[hh:hh + 1], (t_new, d.shape[1])) for hh in range(C_HEADS)], axis=0)

    def wide(x, n):
        return jnp.concatenate([x] * (n // LANES), axis=1)

    @pl.when(ci == 0)
    def _():
        q = q_ref[...]
        qt = jnp.concatenate([q] * C_HEADS, axis=0)
        qrows = jnp.where(own, qt, jnp.zeros_like(qt))
        qrows_ref[...] = qrows
        s_n = lax.dot_general(qrows, kn_ref[...], nt, preferred_element_type=F32)
        s_n = s_n - expand(dkn_ref[...])[:, 0:t_new]
        tpos = lax.broadcasted_iota(jnp.int32, (rows, t_new), 0) & (t_new - 1)
        spos = lax.broadcasted_iota(jnp.int32, (rows, t_new), 1)
        s_n = jnp.where(spos <= tpos, s_n, NEG_INF)
        m0 = jnp.broadcast_to(jnp.max(s_n, axis=-1, keepdims=True), (rows, LANES))
        p_n = jnp.exp2(s_n - m0[:, 0:t_new])
        m_ref[...] = m0
        l_ref[...] = jnp.broadcast_to(jnp.sum(p_n, axis=-1, keepdims=True), (rows, LANES))
        acc_ref[...] = jnp.dot(p_n.astype(BF16), vn_ref[...], preferred_element_type=F32)

    ts = kc_ref.shape[1]
    s = jnp.dot(qrows_ref[...], kc_ref[...].astype(BF16), preferred_element_type=F32)
    s = s - expand(dkc_ref[...])
    m = m_ref[...]
    m_new = jnp.maximum(m, jnp.max(s, axis=-1, keepdims=True))
    alpha = jnp.exp2(m - m_new)
    p = jnp.exp2(s - wide(m_new, ts))
    l_new = alpha * l_ref[...] + jnp.sum(p, axis=-1, keepdims=True)
    acc = wide(alpha, D_C) * acc_ref[...] + lax.dot_general(
        p.astype(BF16), vc_ref[...].astype(BF16), nt, preferred_element_type=F32)
    m_ref[...] = m_new
    l_ref[...] = l_new
    acc_ref[...] = acc

    @pl.when(ci == pl.num_programs(1) - 1)
    def _():
        o_full = jnp.where(own, acc * wide(1.0 / l_new, D_C), 0.0)
        o = o_full[0:t_new]
        for hh in range(1, C_HEADS):
            o = o + o_full[hh * t_new:(hh + 1) * t_new]
        o_ref[...] = o.astype(BF16)


def _att_sample(q, kc, vc, kn, vn, dkc, dkn):
    b, t_new, _ = q.shape
    p_len = kc.shape[2]
    ts = min(SAMPLE_TS, p_len)
    rows = C_HEADS * t_new
    per = lambda bi, ci: (bi, 0, 0)
    cache = pl.BlockSpec((None, D_C, ts), lambda bi, ci: (bi, 0, ci))
    return pl.pallas_call(
        functools.partial(_att_sample_kernel, t_new=t_new),
        grid=(b, p_len // ts),
        in_specs=[pl.BlockSpec((None, t_new, D_C), per), cache, cache,
                  pl.BlockSpec((None, t_new, D_C), per), pl.BlockSpec((None, t_new, D_C), per),
                  pl.BlockSpec((None, C_HEADS, ts), lambda bi, ci: (bi, 0, ci)),
                  pl.BlockSpec((None, C_HEADS, LANES), per)],
        out_specs=pl.BlockSpec((None, t_new, D_C), per),
        out_shape=jax.ShapeDtypeStruct((b, t_new, D_C), BF16),
        scratch_shapes=[pltpu.VMEM((rows, D_C), BF16), pltpu.VMEM((rows, LANES), F32),
                        pltpu.VMEM((rows, LANES), F32), pltpu.VMEM((rows, D_C), F32)],
        compiler_params=_params(("arbitrary", "arbitrary")),
        name="att_sample",
    )(q, kc, vc, kn, vn, dkc, dkn)


def _expand_state(state, t):
    b, _, c = state.shape
    p1 = jnp.zeros((b, t, c), F32).at[:, 0].set(state[:, 1])
    p2 = jnp.zeros((b, t, c), F32).at[:, 0].set(state[:, 0]).at[:, 1].set(state[:, 1])
    return p1.reshape(1, b * t, c), p2.reshape(1, b * t, c)


def kernel(x_prompt, x_sample, state_conv_b, state_ffn, cache_k, cache_v, cache_logf,
           norm_mix, norm_ffn, w_in_ab, sgu_norm, w_spatial, b_spatial, conv_b, w_out_ab,
           w_in_c, b_forget, q_norm, k_norm, w_out_c, w_up, conv_ffn, w_down):
    bp, tp, d = x_prompt.shape
    bs, ts, _ = x_sample.shape
    ns = bs * ts
    xs = x_sample.reshape(1, ns, d)

    pos = jnp.arange(GMLP_CHUNK)
    vis = (pos[None, :] // CHUNK) <= (pos[:, None] // CHUNK)
    w_m = jnp.where(vis[None], w_spatial[0], 0.0)
    wm_p = w_m.astype(BF16)
    bb_p = jnp.broadcast_to(b_spatial[0][:, :, None], (A_GROUPS, GMLP_CHUNK, A_GROUP_DIM))
    reps = GMLP_CHUNK // ts
    eye = jnp.eye(reps, dtype=F32)
    wm_s = jnp.einsum('ab,gts->gatbs', eye, w_m[:, :ts, :ts]).reshape(
        A_GROUPS, GMLP_CHUNK, GMLP_CHUNK).astype(BF16)
    bb_s = jnp.broadcast_to(jnp.tile(b_spatial[0][:, :ts], (1, reps))[:, :, None],
                            (A_GROUPS, GMLP_CHUNK, A_GROUP_DIM))
    g_mix0 = norm_mix[0].reshape(1, d)
    win_ab = w_in_ab[0].astype(BF16)
    sg = sgu_norm[0].reshape(1, D_A)
    cwb = conv_b[0]

    y_p, st_b_p = _mixer_ab(x_prompt, g_mix0, win_ab, sg, wm_p, bb_p, cwb)
    pb1, pb2 = _expand_state(state_conv_b[0], ts)
    y_s, c_s, v_s = _mixer_ab(xs, g_mix0, win_ab, sg, wm_s, bb_s, cwb, seg=ts, p1=pb1, p2=pb2)

    def ffn(layer, x_p, yy_p, x_s, yy_s, w_pre):
        g = norm_ffn[layer].reshape(1, d)
        wup = w_up[layer].astype(BF16)
        wdn = w_down[layer].astype(BF16)
        cw = conv_ffn[layer]
        wpre = w_pre.astype(BF16)
        xo_p, st_p = _post(x_p, yy_p, wpre, g, wup, cw, wdn)
        xo_s, st0_s, st1_s = _post(x_s, yy_s, wpre, g, wup, cw, wdn, seg=ts,
                                   s0=state_ffn[layer, :, 0], s1=state_ffn[layer, :, 1])
        st_s = jnp.stack([st0_s, st1_s], axis=1)
        return xo_p, st_p[:, SUBLANES - (CONV_W - 1):], xo_s, st_s

    xp1, ffn_p0, xs1, ffn_s0 = ffn(0, x_prompt, y_p, xs, y_s, w_out_ab[0])

    g_mix1 = norm_mix[1].reshape(1, d)
    w_c = w_in_c[0]
    wq = w_c[:, 0:D_C].astype(BF16)
    wk = w_c[:, D_C:2 * D_C].astype(BF16)
    wv = w_c[:, 2 * D_C:3 * D_C].astype(BF16)
    wft = w_c[:, 3 * D_C:].T.astype(BF16)
    bf = b_forget[0].reshape(C_HEADS, 1)
    scale = C_HEAD_DIM ** -0.5 * LOG2E
    qg = (jnp.tile(q_norm[0], C_HEADS) * scale).reshape(1, D_C)
    kg = jnp.tile(k_norm[0], C_HEADS)
    kg_rows = jnp.broadcast_to(kg[:, None], (D_C, LANES))

    q_p, kf_p, vf_p, kb_p, vb_p, lf_p = _fox_proj(xp1, g_mix1, wq, wk.T, wv.T, wft, bf, qg,
                                                  kg_rows, transposed=True)
    q_s, kf_s, vf_s, kb_s, vb_s, lf_s = _fox_proj(xs1, g_mix1, wq, wk, wv, wft, bf, qg,
                                                  kg.reshape(1, D_C), transposed=False)

    d_p = _cumsum_lanes(lf_p.reshape(bp * C_HEADS, tp)).reshape(bp, C_HEADS, tp)
    o_p = _att_prompt(q_p, kb_p, vb_p, d_p)

    p_len = cache_k.shape[2]
    lf_s3 = lf_s.reshape(C_HEADS, bs, ts).transpose(1, 0, 2)
    lf_s_pad = jnp.pad(lf_s3, ((0, 0), (0, 0), (0, LANES - ts))).reshape(bs * C_HEADS, LANES)
    dkn = _cumsum_lanes(lf_s_pad).reshape(bs, C_HEADS, LANES)
    lf_c = cache_logf[0].astype(F32).transpose(0, 2, 1).reshape(bs * C_HEADS, p_len)
    dkc = _cumsum_lanes(lf_c, sub_last=True).reshape(bs, C_HEADS, p_len)
    kc = cache_k[0].transpose(0, 2, 3, 1).reshape(bs, D_C, p_len)
    vc = cache_v[0].transpose(0, 2, 3, 1).reshape(bs, D_C, p_len)
    o_s = _att_sample(q_s.reshape(bs, ts, D_C), kc, vc,
                      kb_s.reshape(bs, ts, D_C), vb_s.reshape(bs, ts, D_C), dkc, dkn)

    xp2, ffn_p1, xs2, ffn_s1 = ffn(1, xp1, o_p, xs1, o_s.reshape(1, ns, D_C), w_out_c[0])

    heads = (C_HEADS, C_HEAD_DIM)

    def by_position(xt):
        return xt.reshape(bp, C_HEADS, C_HEAD_DIM, tp).transpose(0, 3, 1, 2)[None]

    return (xp2, xs2.reshape(bs, ts, d),
            st_b_p[None, :, SUBLANES - (CONV_W - 1):],
            c_s.reshape(bs, ts, D_B)[None, :, ts - (CONV_W - 1):],
            v_s.reshape(1, bs, ts, D_A),
            by_position(kf_p), by_position(vf_p),
            lf_p.transpose(0, 2, 1)[None],
            kf_s.reshape(1, bs, ts, *heads), vf_s.reshape(1, bs, ts, *heads),
            lf_s3.transpose(0, 2, 1)[None],
            jnp.stack([ffn_p0, ffn_p1]), jnp.stack([ffn_s0, ffn_s1]))
```

```python
import functools

import jax
import jax.numpy as jnp
from jax import lax
from jax.experimental import pallas as pl
from jax.experimental.pallas import tpu as pltpu

F32 = jnp.float32
BF16 = jnp.bfloat16

D_MODEL = 1024
CHUNK = 64
GMLP_CHUNK = 128
A_GROUP_DIM = 128
D_A = D_MODEL // 2
A_GROUPS = D_A // A_GROUP_DIM
D_B = D_MODEL // 2
CONV_W = 3
C_HEAD_DIM = 64
C_HEADS = D_MODEL // C_HEAD_DIM
D_C = C_HEADS * C_HEAD_DIM
D_FF = 11 * D_MODEL // 4
EPS = 1e-6
NEG_INF = -1e30

LANES = 128
SUBLANES = 8
MXU_COLS = 256
VMEM_LIMIT_BYTES = 56 * 1024 * 1024
ROW_TILE = 1024
FF_CHUNK = MXU_COLS
ATT_TQ = 1024
ATT_HEADS = 4
ATT_TK = MXU_COLS
ATT_BAND = 256
SAMPLE_TS = 2048
LOG2E = 1.4426950408889634


def _params(sem):
    return pltpu.CompilerParams(dimension_semantics=sem, vmem_limit_bytes=VMEM_LIMIT_BYTES)


def _resident(shape):
    nd = len(shape)
    return pl.BlockSpec(shape, lambda *_: (0,) * nd, pipeline_mode=pl.Buffered(1))


def _rms(x, g):
    ms = jnp.mean(x * x, axis=-1, keepdims=True)
    return x * lax.rsqrt(ms + EPS) * g


def _shift_rows_carry(z, carry):
    r1 = pltpu.roll(z, 1, 0)
    r2 = pltpu.roll(z, 2, 0)
    rid = lax.broadcasted_iota(jnp.int32, (SUBLANES, z.shape[1]), 0)
    h1 = jnp.where(rid < 1, pltpu.roll(carry, 1, 0), r1[0:SUBLANES])
    h2 = jnp.where(rid < 2, pltpu.roll(carry, 2, 0), r2[0:SUBLANES])
    z1 = jnp.concatenate([h1, r1[SUBLANES:]], axis=0)
    z2 = jnp.concatenate([h2, r2[SUBLANES:]], axis=0)
    return z1, z2


def _shift_rows_segments(z, p1, p2, seg):
    rid = lax.broadcasted_iota(jnp.int32, z.shape, 0) & (seg - 1)
    z1 = jnp.where(rid >= 1, pltpu.roll(z, 1, 0), p1)
    z2 = jnp.where(rid >= 2, pltpu.roll(z, 2, 0), p2)
    return z1, z2


def _conv3(z, z1, z2, w):
    return z2 * w[0:1] + z1 * w[1:2] + z * w[2:3]


def _ab_kernel(*refs, tm, seg):
    if seg is None:
        (x_ref, g_ref, win_ref, sg_ref, wm_ref, bb_ref, cw_ref,
         y_ref, st_ref, carry_ref) = refs
    else:
        (x_ref, g_ref, win_ref, sg_ref, wm_ref, bb_ref, cw_ref, p1_ref, p2_ref,
         y_ref, c_ref, v_ref) = refs

    h = _rms(x_ref[...], g_ref[...]).astype(BF16)
    z = jnp.dot(h, win_ref[...], preferred_element_type=F32)

    ya = []
    vn = []
    for g in range(A_GROUPS):
        lo = g * A_GROUP_DIM
        u = jax.nn.gelu(z[:, lo:lo + A_GROUP_DIM])
        vg = jax.nn.gelu(z[:, D_A + lo:D_A + lo + A_GROUP_DIM])
        ms = jnp.mean(vg * vg, axis=-1, keepdims=True)
        vg = vg * lax.rsqrt(ms + EPS) * sg_ref[:, lo:lo + A_GROUP_DIM]
        vn.append(vg)
        vb = vg.astype(BF16)
        wm = wm_ref[g]
        bias = bb_ref[g]
        blocks = []
        for c in range(tm // GMLP_CHUNK):
            blk = vb[c * GMLP_CHUNK:(c + 1) * GMLP_CHUNK]
            blocks.append(jnp.dot(wm, blk, preferred_element_type=F32) + bias)
        ya.append(u * jnp.concatenate(blocks, axis=0))

    off = 2 * D_A
    g_b = z[:, off:off + D_B]
    c = z[:, off + D_B:off + 2 * D_B] * z[:, off + 2 * D_B:off + 3 * D_B]
    if seg is None:
        @pl.when(pl.program_id(1) == 0)
        def _():
            carry_ref[...] = jnp.zeros_like(carry_ref)
        c1, c2 = _shift_rows_carry(c, carry_ref[...])
        carry_ref[...] = c[tm - SUBLANES:tm]
        st_ref[...] = c[tm - SUBLANES:tm]
    else:
        c1, c2 = _shift_rows_segments(c, p1_ref[...], p2_ref[...], seg)
        c_ref[...] = c
        v_ref[...] = jnp.concatenate(vn, axis=1)
    yb = g_b * _conv3(c, c1, c2, cw_ref[...])
    y_ref[...] = jnp.concatenate(ya + [yb], axis=1).astype(BF16)


def _mixer_ab(x, g, win, sg, wm, bb, cw, seg=None, p1=None, p2=None):
    b, t, d = x.shape
    tm = min(ROW_TILE, t)
    nt = t // tm
    row = lambda bi, ti: (bi, ti, 0)
    in_specs = [pl.BlockSpec((None, tm, d), row), _resident(g.shape), _resident(win.shape),
                _resident(sg.shape), _resident(wm.shape), _resident(bb.shape), _resident(cw.shape)]
    args = [x, g, win, sg, wm, bb, cw]
    y_shape = jax.ShapeDtypeStruct((b, t, D_A + D_B), BF16)
    y_spec = pl.BlockSpec((None, tm, D_A + D_B), row)
    if seg is None:
        out_shape = (y_shape, jax.ShapeDtypeStruct((b, SUBLANES, D_B), F32))
        out_specs = (y_spec, pl.BlockSpec((None, SUBLANES, D_B), lambda bi, ti: (bi, 0, 0)))
        scratch = [pltpu.VMEM((SUBLANES, D_B), F32)]
    else:
        in_specs += [pl.BlockSpec((None, tm, D_B), row), pl.BlockSpec((None, tm, D_B), row)]
        args += [p1, p2]
        out_shape = (y_shape, jax.ShapeDtypeStruct((b, t, D_B), F32),
                     jax.ShapeDtypeStruct((b, t, D_A), F32))
        out_specs = (y_spec, pl.BlockSpec((None, tm, D_B), row), pl.BlockSpec((None, tm, D_A), row))
        scratch = []
    return pl.pallas_call(
        functools.partial(_ab_kernel, tm=tm, seg=seg),
        grid=(b, nt), in_specs=in_specs, out_specs=out_specs, out_shape=out_shape,
        scratch_shapes=scratch, compiler_params=_params(("arbitrary", "arbitrary")),
        name="mixer_ab_seg" if seg else "mixer_ab",
    )(*args)


def _post_kernel(*refs, tm, seg):
    if seg is None:
        (x_ref, y_ref, wpre_ref, g_ref, wup_ref, cw_ref, wdn_ref,
         xo_ref, st_ref, a_ref, carry_ref) = refs
    else:
        (x_ref, y_ref, wpre_ref, g_ref, wup_ref, cw_ref, wdn_ref, s0_ref, s1_ref,
         xo_ref, st0_ref, st1_ref, a_ref, p1_ref, p2_ref, zs_ref) = refs
        nseg = tm // seg

    x1 = x_ref[...] + jnp.dot(y_ref[...], wpre_ref[...], preferred_element_type=F32)
    h = _rms(x1, g_ref[...]).astype(BF16)

    if seg is None:
        @pl.when(pl.program_id(1) == 0)
        def _():
            carry_ref[...] = jnp.zeros_like(carry_ref)
    else:
        p1_ref[...] = jnp.zeros_like(p1_ref)
        p2_ref[...] = jnp.zeros_like(p2_ref)

    def conv_cols(lo):
        z = jnp.dot(h, wup_ref[:, lo:lo + FF_CHUNK], preferred_element_type=F32)
        if seg is None:
            z1, z2 = _shift_rows_carry(z, carry_ref[:, lo:lo + FF_CHUNK])
            carry_ref[:, lo:lo + FF_CHUNK] = z[tm - SUBLANES:tm]
            st_ref[:, lo:lo + FF_CHUNK] = z[tm - SUBLANES:tm]
        else:
            for i in range(FF_CHUNK // LANES):
                cols = slice(lo + i * LANES, lo + (i + 1) * LANES)
                p1_ref[i, pl.ds(0, nseg, stride=seg), :] = s1_ref[:, cols]
                p2_ref[i, pl.ds(0, nseg, stride=seg), :] = s0_ref[:, cols]
                p2_ref[i, pl.ds(1, nseg, stride=seg), :] = s1_ref[:, cols]
                zs_ref[i] = z[:, i * LANES:(i + 1) * LANES]
                st0_ref[:, cols] = zs_ref[i, pl.ds(seg - 2, nseg, stride=seg), :]
                st1_ref[:, cols] = zs_ref[i, pl.ds(seg - 1, nseg, stride=seg), :]
            p1 = jnp.concatenate([p1_ref[i] for i in range(FF_CHUNK // LANES)], axis=1)
            p2 = jnp.concatenate([p2_ref[i] for i in range(FF_CHUNK // LANES)], axis=1)
            z1, z2 = _shift_rows_segments(z, p1, p2, seg)
        return _conv3(z, z1, z2, cw_ref[:, lo:lo + FF_CHUNK])

    for j in range(D_FF // FF_CHUNK):
        gate = conv_cols(j * FF_CHUNK)
        up = conv_cols(D_FF + j * FF_CHUNK)
        a_ref[:, j * FF_CHUNK:(j + 1) * FF_CHUNK] = (jax.nn.silu(gate) * up).astype(BF16)

    xo_ref[...] = x1 + jnp.dot(a_ref[...], wdn_ref[...], preferred_element_type=F32)


def _post(x, y, wpre, g, wup, cw, wdn, seg=None, s0=None, s1=None):
    b, t, d = x.shape
    tm = min(ROW_TILE, t)
    nt = t // tm
    assert seg is None or (b == 1 and nt == 1), "packed streams must fit one row tile"
    row = lambda bi, ti: (bi, ti, 0)
    in_specs = [pl.BlockSpec((None, tm, d), row), pl.BlockSpec((None, tm, y.shape[2]), row),
                _resident(wpre.shape), _resident(g.shape), _resident(wup.shape),
                _resident(cw.shape), _resident(wdn.shape)]
    args = [x, y, wpre, g, wup, cw, wdn]
    xo_shape = jax.ShapeDtypeStruct((b, t, d), F32)
    xo_spec = pl.BlockSpec((None, tm, d), row)
    scratch = [pltpu.VMEM((tm, D_FF), BF16)]
    if seg is None:
        out_shape = (xo_shape, jax.ShapeDtypeStruct((b, SUBLANES, 2 * D_FF), F32))
        out_specs = (xo_spec, pl.BlockSpec((None, SUBLANES, 2 * D_FF), lambda bi, ti: (bi, 0, 0)))
        scratch.append(pltpu.VMEM((SUBLANES, 2 * D_FF), F32))
    else:
        in_specs += [_resident(s0.shape), _resident(s1.shape)]
        args += [s0, s1]
        st_shape = jax.ShapeDtypeStruct(s0.shape, F32)
        st_spec = pl.BlockSpec(s0.shape, lambda bi, ti: (0, 0))
        out_shape = (xo_shape, st_shape, st_shape)
        out_specs = (xo_spec, st_spec, st_spec)
        scratch += [pltpu.VMEM((FF_CHUNK // LANES, tm, LANES), F32)] * 3
    return pl.pallas_call(
        functools.partial(_post_kernel, tm=tm, seg=seg),
        grid=(b, nt), in_specs=in_specs, out_specs=out_specs, out_shape=out_shape,
        scratch_shapes=scratch, compiler_params=_params(("arbitrary", "arbitrary")),
        name="post_ffn_seg" if seg else "post_ffn",
    )(*args)


def _fox_proj_kernel(x_ref, g_ref, wq_ref, wk_ref, wv_ref, wf_ref, bf_ref, qg_ref, kg_ref,
                     q_ref, kf_ref, vf_ref, kb_ref, vb_ref, lf_ref, *, transposed):
    h = _rms(x_ref[...], g_ref[...]).astype(BF16)
    nt = (((1,), (1,)), ((), ()))
    low = lax.broadcasted_iota(jnp.int32, (1, LANES), 1) < C_HEAD_DIM

    def head_norm(x, gain):
        out = []
        for c in range(D_C // LANES):
            xb = x[:, c * LANES:(c + 1) * LANES]
            sq = xb * xb
            lo = jnp.sum(jnp.where(low, sq, 0.0), axis=-1, keepdims=True)
            hi = jnp.sum(jnp.where(low, 0.0, sq), axis=-1, keepdims=True)
            ms = jnp.where(low, lo, hi) * (1.0 / C_HEAD_DIM)
            out.append(xb * lax.rsqrt(ms + EPS) * gain[:, c * LANES:(c + 1) * LANES])
        return jnp.concatenate(out, axis=1)

    q = jnp.dot(h, wq_ref[...], preferred_element_type=F32)
    q_ref[...] = head_norm(q, qg_ref[...]).astype(BF16)
    if transposed:
        kt = lax.dot_general(wk_ref[...], h, nt, preferred_element_type=F32)
        tm = kt.shape[1]
        k3 = kt.reshape(C_HEADS, C_HEAD_DIM, tm)
        ms = jnp.mean(k3 * k3, axis=1, keepdims=True)
        kn = (k3 * lax.rsqrt(ms + EPS)).reshape(D_C, tm)
        kn = kn * jnp.concatenate([kg_ref[...]] * (tm // LANES), axis=1)
        v = lax.dot_general(wv_ref[...], h, nt, preferred_element_type=F32)
    else:
        kn = head_norm(jnp.dot(h, wk_ref[...], preferred_element_type=F32), kg_ref[...])
        v = jnp.dot(h, wv_ref[...], preferred_element_type=F32)
    kf_ref[...] = kn
    vf_ref[...] = v
    kb_ref[...] = kn.astype(BF16)
    vb_ref[...] = v.astype(BF16)
    f = lax.dot_general(wf_ref[...], h, nt, preferred_element_type=F32) + bf_ref[...]
    lf_ref[...] = jnp.minimum(f, 0.0) - jnp.log1p(jnp.exp(-jnp.abs(f)))


def _fox_proj(x, g, wq, wk, wv, wft, bf, qg, kg, transposed):
    b, t, d = x.shape
    tm = min(ROW_TILE, t)
    row = pl.BlockSpec((None, tm, D_C), lambda bi, ti: (bi, ti, 0))
    if transposed:
        kv = pl.BlockSpec((None, D_C, tm), lambda bi, ti: (bi, 0, ti))
        kv_shape = (b, D_C, t)
    else:
        kv = row
        kv_shape = (b, t, D_C)
    weights = [g, wq, wk, wv, wft, bf, qg, kg]
    return pl.pallas_call(
        functools.partial(_fox_proj_kernel, transposed=transposed),
        grid=(b, t // tm),
        in_specs=[pl.BlockSpec((None, tm, d), lambda bi, ti: (bi, ti, 0))]
        + [_resident(w.shape) for w in weights],
        out_specs=(row, kv, kv, kv, kv,
                   pl.BlockSpec((None, C_HEADS, tm), lambda bi, ti: (bi, 0, ti))),
        out_shape=(jax.ShapeDtypeStruct((b, t, D_C), BF16),
                   jax.ShapeDtypeStruct(kv_shape, F32), jax.ShapeDtypeStruct(kv_shape, F32),
                   jax.ShapeDtypeStruct(kv_shape, BF16), jax.ShapeDtypeStruct(kv_shape, BF16),
                   jax.ShapeDtypeStruct((b, C_HEADS, t), F32)),
        compiler_params=_params(("arbitrary", "arbitrary")),
        name="fox_proj_t" if transposed else "fox_proj",
    )(x, *weights)


def _cumsum_kernel(x_ref, tri_ref, o_ref, *, sub_last):
    r, n = x_ref.shape
    tri = tri_ref[...]
    off = jnp.zeros((r, 1), F32)
    for i in range(n // LANES):
        xb = x_ref[:, i * LANES:(i + 1) * LANES] * LOG2E
        hi = xb.astype(BF16)
        r1 = xb - hi.astype(F32)
        mid = r1.astype(BF16)
        lo = (r1 - mid.astype(F32)).astype(BF16)
        c = (jnp.dot(hi, tri, preferred_element_type=F32)
             + jnp.dot(mid, tri, preferred_element_type=F32)
             + jnp.dot(lo, tri, preferred_element_type=F32)) + off
        o_ref[:, i * LANES:(i + 1) * LANES] = c
        off = c[:, LANES - 1:LANES]
    if sub_last:
        o_ref[...] = o_ref[...] - off


def _cumsum_lanes(x, sub_last=False):
    tri = (jnp.arange(LANES)[:, None] <= jnp.arange(LANES)[None, :]).astype(BF16)
    return pl.pallas_call(
        functools.partial(_cumsum_kernel, sub_last=sub_last),
        out_shape=jax.ShapeDtypeStruct(x.shape, F32),
        compiler_params=pltpu.CompilerParams(vmem_limit_bytes=VMEM_LIMIT_BYTES),
        name="cumsum_lanes",
    )(x, tri)


def _att_prompt_kernel(q_ref, k_ref, v_ref, dk_ref, tri_ref, o_ref,
                       v0_ref, v1_ref, q0_ref, q1_ref, m_ref, acc_ref):
    hq = pl.program_id(1)
    qi = pl.program_id(2)
    tq = q_ref.shape[0]
    width = q_ref.shape[1]
    pairs = width // LANES
    low = (lax.broadcasted_iota(jnp.int32, (1, width), 1) & (LANES - 1)) < C_HEAD_DIM
    low1 = low[:, 0:LANES]
    low_rows = (lax.broadcasted_iota(jnp.int32, (width, 1), 0) & (LANES - 1)) < C_HEAD_DIM

    @pl.when(qi == 0)
    def _():
        v = v_ref[...]
        one = jnp.ones_like(v)
        v0_ref[...] = jnp.where(low_rows, v, one)
        v1_ref[...] = jnp.where(low_rows, one, v)

    q = q_ref[...]
    zero = jnp.zeros_like(q)
    q0_ref[...] = jnp.where(low, q, zero)
    q1_ref[...] = jnp.where(low, zero, q)
    m_ref[...] = jnp.full(m_ref.shape, NEG_INF, F32)
    acc_ref[...] = jnp.zeros_like(acc_ref)
    qh = (q0_ref, q1_ref)
    vh = (v0_ref, v1_ref)
    nt = (((1,), (1,)), ((), ()))

    def scores(c, ks, span, r0, r1, causal):
        pr, e = divmod(c, 2)
        cols = slice(pr * LANES, (pr + 1) * LANES)
        s = jnp.dot(qh[e][r0:r1, cols], k_ref[cols, pl.ds(ks, span)],
                    preferred_element_type=F32)
        s = s - dk_ref[pl.ds(2 * pairs * hq + c, 1), pl.ds(ks, span)]
        if causal:
            band = r1 - r0
            last = s[:, span - band:] + tri_ref[...]
            s = last if span == band else jnp.concatenate([s[:, :span - band], last], axis=1)
        return s

    def absorb(c, ks, span, r0, r1, s):
        pr, e = divmod(c, 2)
        cols = slice(pr * LANES, (pr + 1) * LANES)
        m = m_ref[c, r0:r1, :]
        m_new = jnp.maximum(m, jnp.max(s, axis=-1, keepdims=True))
        alpha = jnp.exp2(m - m_new)
        p = jnp.exp2(s - jnp.concatenate([m_new] * (span // LANES), axis=1)).astype(BF16)
        pv = lax.dot_general(p, vh[e][cols, pl.ds(ks, span)], nt, preferred_element_type=F32)
        acc_ref[c, r0:r1, :] = alpha * acc_ref[c, r0:r1, :] + pv
        m_ref[c, r0:r1, :] = m_new

    per_tile = tq // ATT_TK

    def full_blocks(j, _):
        for i in range(per_tile):
            ks = pl.multiple_of((j * per_tile + i) * ATT_TK, ATT_TK)
            for c in range(2 * pairs):
                absorb(c, ks, ATT_TK, 0, tq, scores(c, ks, ATT_TK, 0, tq, False))
        return 0

    lax.fori_loop(0, qi, full_blocks, 0)
    band = tri_ref.shape[0]
    ks0 = pl.multiple_of(qi * tq, ATT_TK)
    steps = [(c, ks0, (i + 1) * band, i * band, (i + 1) * band)
             for i in range(tq // band) for c in range(2 * pairs)]
    ahead = scores(*steps[0], True)
    for n, step in enumerate(steps):
        s = ahead
        if n + 1 < len(steps):
            ahead = scores(*steps[n + 1], True)
        absorb(*step, s)
    for pr in range(pairs):
        acc0, acc1 = acc_ref[2 * pr], acc_ref[2 * pr + 1]
        o0 = acc0 * (1.0 / acc0[:, C_HEAD_DIM:C_HEAD_DIM + 1])
        o1 = acc1 * (1.0 / acc1[:, 0:1])
        o_ref[:, pr * LANES:(pr + 1) * LANES] = jnp.where(low1, o0, o1).astype(BF16)


def _att_prompt(q, k, v, dk):
    b, t, _ = q.shape
    tq = min(ATT_TQ, t)
    nq = t // tq
    width = ATT_HEADS * C_HEAD_DIM
    pos = jnp.arange(min(ATT_BAND, tq))
    tri = jnp.where(pos[None, :] <= pos[:, None], 0.0, NEG_INF).astype(F32)
    return pl.pallas_call(
        _att_prompt_kernel,
        grid=(b, C_HEADS // ATT_HEADS, nq),
        in_specs=[pl.BlockSpec((None, tq, width), lambda bi, hi, qi: (bi, qi, hi)),
                  pl.BlockSpec((None, width, t), lambda bi, hi, qi: (bi, hi, 0)),
                  pl.BlockSpec((None, width, t), lambda bi, hi, qi: (bi, hi, 0)),
                  pl.BlockSpec((None, C_HEADS, t), lambda bi, hi, qi: (bi, 0, 0)),
                  _resident(tri.shape)],
        out_specs=pl.BlockSpec((None, tq, width), lambda bi, hi, qi: (bi, qi, hi)),
        out_shape=jax.ShapeDtypeStruct((b, t, D_C), BF16),
        scratch_shapes=[pltpu.VMEM((width, t), BF16), pltpu.VMEM((width, t), BF16),
                        pltpu.VMEM((tq, width), BF16), pltpu.VMEM((tq, width), BF16),
                        pltpu.VMEM((ATT_HEADS, tq, LANES), F32),
                        pltpu.VMEM((ATT_HEADS, tq, LANES), F32)],
        compiler_params=_params(("arbitrary", "arbitrary", "arbitrary")),
        name="att_prompt",
    )(q, k, v, dk, tri)


def _att_sample_kernel(q_ref, kc_ref, vc_ref, kn_ref, vn_ref, dkc_ref, dkn_ref, o_ref,
                       qrows_ref, m_ref, l_ref, acc_ref, *, t_new):
    ci = pl.program_id(1)
    rows = C_HEADS * t_new
    rhead = lax.broadcasted_iota(jnp.int32, (rows, D_C), 0) // t_new
    lhead = lax.broadcasted_iota(jnp.int32, (rows, D_C), 1) // C_HEAD_DIM
    own = rhead == lhead
    nt = (((1,), (1,)), ((), ()))

    def expand(d):
        return jnp.concatenate(
            [jnp.broadcast_to(d[hh:hh + 1], (t_new, d.shape[1])) for hh in range(C_HEADS)], axis=0)

    def wide(x, n):
        return jnp.concatenate([x] * (n // LANES), axis=1)

    @pl.when(ci == 0)
    def _():
        q = q_ref[...]
        qt = jnp.concatenate([q] * C_HEADS, axis=0)
        qrows = jnp.where(own, qt, jnp.zeros_like(qt))
        qrows_ref[...] = qrows
        s_n = lax.dot_general(qrows, kn_ref[...], nt, preferred_element_type=F32)
        s_n = s_n - expand(dkn_ref[...])[:, 0:t_new]
        tpos = lax.broadcasted_iota(jnp.int32, (rows, t_new), 0) & (t_new - 1)
        spos = lax.broadcasted_iota(jnp.int32, (rows, t_new), 1)
        s_n = jnp.where(spos <= tpos, s_n, NEG_INF)
        m0 = jnp.broadcast_to(jnp.max(s_n, axis=-1, keepdims=True), (rows, LANES))
        p_n = jnp.exp2(s_n - m0[:, 0:t_new])
        m_ref[...] = m0
        l_ref[...] = jnp.broadcast_to(jnp.sum(p_n, axis=-1, keepdims=True), (rows, LANES))
        acc_ref[...] = jnp.dot(p_n.astype(BF16), vn_ref[...], preferred_element_type=F32)

    ts = kc_ref.shape[1]
    s = jnp.dot(qrows_ref[...], kc_ref[...].astype(BF16), preferred_element_type=F32)
    s = s - expand(dkc_ref[...])
    m = m_ref[...]
    m_new = jnp.maximum(m, jnp.max(s, axis=-1, keepdims=True))
    alpha = jnp.exp2(m - m_new)
    p = jnp.exp2(s - wide(m_new, ts))
    l_new = alpha * l_ref[...] + jnp.sum(p, axis=-1, keepdims=True)
    acc = wide(alpha, D_C) * acc_ref[...] + lax.dot_general(
        p.astype(BF16), vc_ref[...].astype(BF16), nt, preferred_element_type=F32)
    m_ref[...] = m_new
    l_ref[...] = l_new
    acc_ref[...] = acc

    @pl.when(ci == pl.num_programs(1) - 1)
    def _():
        o_full = jnp.where(own, acc * wide(1.0 / l_new, D_C), 0.0)
        o = o_full[0:t_new]
        for hh in range(1, C_HEADS):
            o = o + o_full[hh * t_new:(hh + 1) * t_new]
        o_ref[...] = o.astype(BF16)


def _att_sample(q, kc, vc, kn, vn, dkc, dkn):
    b, t_new, _ = q.shape
    p_len = kc.shape[2]
    ts = min(SAMPLE_TS, p_len)
    rows = C_HEADS * t_new
    per = lambda bi, ci: (bi, 0, 0)
    cache = pl.BlockSpec((None, D_C, ts), lambda bi, ci: (bi, 0, ci))
    return pl.pallas_call(
        functools.partial(_att_sample_kernel, t_new=t_new),
        grid=(b, p_len // ts),
        in_specs=[pl.BlockSpec((None, t_new, D_C), per), cache, cache,
                  pl.BlockSpec((None, t_new, D_C), per), pl.BlockSpec((None, t_new, D_C), per),
                  pl.BlockSpec((None, C_HEADS, ts), lambda bi, ci: (bi, 0, ci)),
                  pl.BlockSpec((None, C_HEADS, LANES), per)],
        out_specs=pl.BlockSpec((None, t_new, D_C), per),
        out_shape=jax.ShapeDtypeStruct((b, t_new, D_C), BF16),
        scratch_shapes=[pltpu.VMEM((rows, D_C), BF16), pltpu.VMEM((rows, LANES), F32),
                        pltpu.VMEM((rows, LANES), F32), pltpu.VMEM((rows, D_C), F32)],
        compiler_params=_params(("arbitrary", "arbitrary")),
        name="att_sample",
    )(q, kc, vc, kn, vn, dkc, dkn)


def _expand_state(state, t):
    b, _, c = state.shape
    p1 = jnp.zeros((b, t, c), F32).at[:, 0].set(state[:, 1])
    p2 = jnp.zeros((b, t, c), F32).at[:, 0].set(state[:, 0]).at[:, 1].set(state[:, 1])
    return p1.reshape(1, b * t, c), p2.reshape(1, b * t, c)


def kernel(x_prompt, x_sample, state_conv_b, state_ffn, cache_k, cache_v, cache_logf,
           norm_mix, norm_ffn, w_in_ab, sgu_norm, w_spatial, b_spatial, conv_b, w_out_ab,
           w_in_c, b_forget, q_norm, k_norm, w_out_c, w_up, conv_ffn, w_down):
    bp, tp, d = x_prompt.shape
    bs, ts, _ = x_sample.shape
    ns = bs * ts
    xs = x_sample.reshape(1, ns, d)

    pos = jnp.arange(GMLP_CHUNK)
    vis = (pos[None, :] // CHUNK) <= (pos[:, None] // CHUNK)
    w_m = jnp.where(vis[None], w_spatial[0], 0.0)
    wm_p = w_m.astype(BF16)
    bb_p = jnp.broadcast_to(b_spatial[0][:, :, None], (A_GROUPS, GMLP_CHUNK, A_GROUP_DIM))
    reps = GMLP_CHUNK // ts
    eye = jnp.eye(reps, dtype=F32)
    wm_s = jnp.einsum('ab,gts->gatbs', eye, w_m[:, :ts, :ts]).reshape(
        A_GROUPS, GMLP_CHUNK, GMLP_CHUNK).astype(BF16)
    bb_s = jnp.broadcast_to(jnp.tile(b_spatial[0][:, :ts], (1, reps))[:, :, None],
                            (A_GROUPS, GMLP_CHUNK, A_GROUP_DIM))
    g_mix0 = norm_mix[0].reshape(1, d)
    win_ab = w_in_ab[0].astype(BF16)
    sg = sgu_norm[0].reshape(1, D_A)
    cwb = conv_b[0]

    y_p, st_b_p = _mixer_ab(x_prompt, g_mix0, win_ab, sg, wm_p, bb_p, cwb)
    pb1, pb2 = _expand_state(state_conv_b[0], ts)
    y_s, c_s, v_s = _mixer_ab(xs, g_mix0, win_ab, sg, wm_s, bb_s, cwb, seg=ts, p1=pb1, p2=pb2)

    def ffn(layer, x_p, yy_p, x_s, yy_s, w_pre):
        g = norm_ffn[layer].reshape(1, d)
        wup = w_up[layer].astype(BF16)
        wdn = w_down[layer].astype(BF16)
        cw = conv_ffn[layer]
        wpre = w_pre.astype(BF16)
        xo_p, st_p = _post(x_p, yy_p, wpre, g, wup, cw, wdn)
        xo_s, st0_s, st1_s = _post(x_s, yy_s, wpre, g, wup, cw, wdn, seg=ts,
                                   s0=state_ffn[layer, :, 0], s1=state_ffn[layer, :, 1])
        st_s = jnp.stack([st0_s, st1_s], axis=1)
        return xo_p, st_p[:, SUBLANES - (CONV_W - 1):], xo_s, st_s

    xp1, ffn_p0, xs1, ffn_s0 = ffn(0, x_prompt, y_p, xs, y_s, w_out_ab[0])

    g_mix1 = norm_mix[1].reshape(1, d)
    w_c = w_in_c[0]
    wq = w_c[:, 0:D_C].astype(BF16)
    wk = w_c[:, D_C:2 * D_C].astype(BF16)
    wv = w_c[:, 2 * D_C:3 * D_C].astype(BF16)
    wft = w_c[:, 3 * D_C:].T.astype(BF16)
    bf = b_forget[0].reshape(C_HEADS, 1)
    scale = C_HEAD_DIM ** -0.5 * LOG2E
    qg = (jnp.tile(q_norm[0], C_HEADS) * scale).reshape(1, D_C)
    kg = jnp.tile(k_norm[0], C_HEADS)
    kg_rows = jnp.broadcast_to(kg[:, None], (D_C, LANES))

    q_p, kf_p, vf_p, kb_p, vb_p, lf_p = _fox_proj(xp1, g_mix1, wq, wk.T, wv.T, wft, bf, qg,
                                                  kg_rows, transposed=True)
    q_s, kf_s, vf_s, kb_s, vb_s, lf_s = _fox_proj(xs1, g_mix1, wq, wk, wv, wft, bf, qg,
                                                  kg.reshape(1, D_C), transposed=False)

    d_p = _cumsum_lanes(lf_p.reshape(bp * C_HEADS, tp)).reshape(bp, C_HEADS, tp)
    o_p = _att_prompt(q_p, kb_p, vb_p, d_p)

    p_len = cache_k.shape[2]
    lf_s3 = lf_s.reshape(C_HEADS, bs, ts).transpose(1, 0, 2)
    lf_s_pad = jnp.pad(lf_s3, ((0, 0), (0, 0), (0, LANES - ts))).reshape(bs * C_HEADS, LANES)
    dkn = _cumsum_lanes(lf_s_pad).reshape(bs, C_HEADS, LANES)
    lf_c = cache_logf[0].astype(F32).transpose(0, 2, 1).reshape(bs * C_HEADS, p_len)
    dkc = _cumsum_lanes(lf_c, sub_last=True).reshape(bs, C_HEADS, p_len)
    kc = cache_k[0].transpose(0, 2, 3, 1).reshape(bs, D_C, p_len)
    vc = cache_v[0].transpose(0, 2, 3, 1).reshape(bs, D_C, p_len)
    o_s = _att_sample(q_s.reshape(bs, ts, D_C), kc, vc,
                      kb_s.reshape(bs, ts, D_C), vb_s.reshape(bs, ts, D_C), dkc, dkn)

    xp2, ffn_p1, xs2, ffn_s1 = ffn(1, xp1, o_p, xs1, o_s.reshape(1, ns, D_C), w_out_c[0])

    heads = (C_HEADS, C_HEAD_DIM)

    def by_position(xt):
        return xt.reshape(bp, C_HEADS, C_HEAD_DIM, tp).transpose(0, 3, 1, 2)[None]

    return (xp2, xs2.reshape(bs, ts, d),
            st_b_p[None, :, SUBLANES - (CONV_W - 1):],
            c_s.reshape(bs, ts, D_B)[None, :, ts - (CONV_W - 1):],
            v_s.reshape(1, bs, ts, D_A),
            by_position(kf_p), by_position(vf_p),
            lf_p.transpose(0, 2, 1)[None],
            kf_s.reshape(1, bs, ts, *heads), vf_s.reshape(1, bs, ts, *heads),
            lf_s3.transpose(0, 2, 1)[None],
            jnp.stack([ffn_p0, ffn_p1]), jnp.stack([ffn_s0, ffn_s1]))
```
